```python
import numpy as np
import jax
import jax.numpy as jnp
from jax import lax

D_MODEL = 1024
BATCH = 8
SEQ = 2048
DEPTH = 2
DEC_BATCH = 128
DEC_SEQ = 4
PAST_LEN = 16384
PAGE_SIZE = 128

GDN_HEADS = 4
GDN_DK = 128
GDN_DV = 128
GLA_HEADS = 4
GLA_DK = 64
GLA_DV = 128
GLA_GATE_RANK = 16
GLA_GATE_NORMALIZER = 16.0
SSD_HEADS = 8
SSD_HEADDIM = 64
SSD_GROUPS = 2
SSD_DSTATE = 128
CONV_WIDTH = 4
CHUNK = 64
D_FF = 2816
N_BRANCHES = 3
NORM_EPS = 1e-6

GDN_QK = GDN_HEADS * GDN_DK
GDN_V = GDN_HEADS * GDN_DV
GDN_CONV_DIM = 2 * GDN_QK + GDN_V
GLA_QK = GLA_HEADS * GLA_DK
GLA_V = GLA_HEADS * GLA_DV
SSD_INNER = SSD_HEADS * SSD_HEADDIM
SSD_BC = SSD_GROUPS * SSD_DSTATE
SSD_CONV_DIM = SSD_INNER + 2 * SSD_BC
IN_SIZES = (GDN_CONV_DIM, GDN_V, GDN_HEADS, GDN_HEADS,
            GLA_QK, GLA_QK, GLA_V, GLA_V, GLA_GATE_RANK,
            SSD_INNER, SSD_CONV_DIM, SSD_HEADS,
            N_BRANCHES * D_MODEL)
D_IN = sum(IN_SIZES)

kernel_name = 'hybrid_gdn_gla_ssd_macaron_step'


def _rmsnorm(x, w):
    xf = x.astype(jnp.float32)
    y = xf * lax.rsqrt(jnp.mean(xf * xf, axis=-1, keepdims=True) + NORM_EPS)
    return (y * w.astype(jnp.float32)).astype(x.dtype)


def _l2norm(x):
    xf = x.astype(jnp.float32)
    return xf * lax.rsqrt(jnp.sum(xf * xf, axis=-1, keepdims=True) + NORM_EPS)


def _swiglu(x, w_gate, w_up, w_down):
    return (jax.nn.silu(x @ w_gate) * (x @ w_up)) @ w_down


def _causal_conv(x, buf, w, b=None):
    T = x.shape[1]
    xp = jnp.concatenate([buf.astype(x.dtype), x], axis=1)
    y = xp[:, 0:T] * w[:, 0]
    for i in range(1, CONV_WIDTH):
        y = y + xp[:, i:i + T] * w[:, i]
    if b is not None:
        y = y + b
    return jax.nn.silu(y), xp[:, T:]


def _chunked(x, L):
    Bsz, H, T = x.shape[:3]
    n = -(-T // L)
    x = jnp.pad(x, [(0, 0), (0, 0), (0, n * L - T)] + [(0, 0)] * (x.ndim - 3))
    x = x.reshape((Bsz, H, n, L) + x.shape[3:])
    return jnp.moveaxis(x, 2, 0)


def _unchunk(y, T):
    y = jnp.moveaxis(y, 0, 2)
    y = y.reshape(y.shape[:2] + (-1,) + y.shape[4:])[:, :, :T]
    return jnp.moveaxis(y, 1, 2)


def _heads_first(*arrs):
    return tuple(jnp.moveaxis(a, 2, 1) for a in arrs)


def _segment_decay(G, mask):
    diff = G[..., :, None] - G[..., None, :]
    return jnp.where(mask, jnp.exp(jnp.where(mask, diff, 0.0)), 0.0)


def _gated_delta_rule(q, k, v, g, beta, S0):
    T = q.shape[1]
    L = min(CHUNK, T)
    xs = tuple(_chunked(a, L) for a in _heads_first(q, k, v, g, beta))
    causal = jnp.tril(jnp.ones((L, L), dtype=bool))
    strict = jnp.tril(jnp.ones((L, L), dtype=bool), -1)
    eye = jnp.eye(L, dtype=jnp.float32)

    def step(S, inp):
        qc, kc, vc, gc, bc = inp
        G = jnp.cumsum(gc, axis=-1)
        dec = _segment_decay(G, causal)
        A = jnp.where(strict, bc[..., :, None] * jnp.einsum('bhid,bhjd->bhij', kc, kc) * dec, 0.0)
        rhs = jnp.concatenate([bc[..., None] * vc, (bc * jnp.exp(G))[..., None] * kc], axis=-1)
        sol = lax.linalg.triangular_solve(eye + A, rhs, left_side=True, lower=True)
        u = sol[..., :GDN_DV] - jnp.einsum('bhik,bhkv->bhiv', sol[..., GDN_DV:], S)
        o = (jnp.einsum('bhik,bhkv->bhiv', qc * jnp.exp(G)[..., None], S)
             + jnp.einsum('bhij,bhjv->bhiv', jnp.einsum('bhid,bhjd->bhij', qc, kc) * dec, u))
        G_last = G[..., -1:]
        S = (jnp.exp(G_last)[..., None] * S
             + jnp.einsum('bhjk,bhjv->bhkv', kc * jnp.exp(G_last - G)[..., None], u))
        return S, o

    S, o = lax.scan(step, S0, xs)
    return _unchunk(o, T), S


def _gla(q, k, v, gk, S0):
    T = q.shape[1]
    L = min(CHUNK, T)
    xs = tuple(_chunked(a, L) for a in _heads_first(q, k, v, gk))
    causal = jnp.tril(jnp.ones((L, L), dtype=bool))[:, :, None]

    def step(S, inp):
        qc, kc, vc, gc = inp
        Bc = jnp.cumsum(gc, axis=2)
        diff = Bc[:, :, :, None, :] - Bc[:, :, None, :, :]
        dec = jnp.where(causal, jnp.exp(jnp.where(causal, diff, 0.0)), 0.0)
        att = jnp.sum(qc[:, :, :, None, :] * kc[:, :, None, :, :] * dec, axis=-1)
        o = (jnp.einsum('bhik,bhkv->bhiv', qc * jnp.exp(Bc), S)
             + jnp.einsum('bhij,bhjv->bhiv', att, vc))
        B_last = Bc[:, :, -1:, :]
        S = (jnp.exp(B_last[:, :, 0, :])[..., None] * S
             + jnp.einsum('bhjk,bhjv->bhkv', kc * jnp.exp(B_last - Bc), vc))
        return S, o

    S, o = lax.scan(step, S0, xs)
    return _unchunk(o, T), S


def _ssd(x, dt, a, Bm, Cm, S0):
    T = x.shape[1]
    L = min(CHUNK, T)
    xs = tuple(_chunked(t, L) for t in _heads_first(x, dt, a, Bm, Cm))
    causal = jnp.tril(jnp.ones((L, L), dtype=bool))

    def step(S, inp):
        xc, dtc, ac, bc, cc = inp
        A = jnp.cumsum(ac, axis=-1)
        dec = _segment_decay(A, causal)
        xdt = xc * dtc[..., None]
        att = jnp.einsum('bhin,bhjn->bhij', cc, bc) * dec
        y = (jnp.einsum('bhij,bhjp->bhip', att, xdt)
             + jnp.einsum('bhin,bhpn->bhip', cc * jnp.exp(A)[..., None], S))
        A_last = A[..., -1:]
        S = (jnp.exp(A_last)[..., None] * S
             + jnp.einsum('bhjp,bhjn->bhpn', xdt * jnp.exp(A_last - A)[..., None], bc))
        return S, y

    S, y = lax.scan(step, S0, xs)
    return _unchunk(y, T), S


def _token_mixer(h, conv_gdn, s_gdn, s_gla, conv_ssd, s_ssd, p):
    f32 = jnp.float32
    Bsz, T, _ = h.shape
    proj = h @ p['w_in']
    (gdn_qkv, gdn_z, gdn_a, gdn_b, gla_q, gla_k, gla_v, gla_g, gla_r,
     ssd_z, ssd_xbc, ssd_dt, gate_logits) = jnp.split(
        proj, np.cumsum(IN_SIZES)[:-1].tolist(), axis=-1)

    qkv, conv_gdn_new = _causal_conv(gdn_qkv, conv_gdn, p['gdn_conv_w'])
    q, k, v = jnp.split(qkv, [GDN_QK, 2 * GDN_QK], axis=-1)
    q = _l2norm(q.reshape(Bsz, T, GDN_HEADS, GDN_DK)) * (GDN_DK ** -0.5)
    k = _l2norm(k.reshape(Bsz, T, GDN_HEADS, GDN_DK))
    v = v.reshape(Bsz, T, GDN_HEADS, GDN_DV).astype(f32)
    beta = jax.nn.sigmoid(gdn_b.astype(f32))
    g = -jnp.exp(p['gdn_A_log'].astype(f32)) * jax.nn.softplus(gdn_a.astype(f32) + p['gdn_dt_bias'].astype(f32))
    o, s_gdn_new = _gated_delta_rule(q, k, v, g, beta, s_gdn.astype(f32))
    o = _rmsnorm(o, p['gdn_norm']) * jax.nn.silu(gdn_z.reshape(Bsz, T, GDN_HEADS, GDN_DV).astype(f32))
    o_gdn = o.reshape(Bsz, T, GDN_V).astype(h.dtype)

    gk = jax.nn.log_sigmoid((gla_r @ p['gla_gate_up'] + p['gla_gate_bias']).astype(f32)) / GLA_GATE_NORMALIZER
    o, s_gla_new = _gla(gla_q.reshape(Bsz, T, GLA_HEADS, GLA_DK).astype(f32) * (GLA_DK ** -0.5),
                        gla_k.reshape(Bsz, T, GLA_HEADS, GLA_DK).astype(f32),
                        gla_v.reshape(Bsz, T, GLA_HEADS, GLA_DV).astype(f32),
                        gk.reshape(Bsz, T, GLA_HEADS, GLA_DK),
                        s_gla.astype(f32))
    o = _rmsnorm(o, p['gla_norm']) * jax.nn.silu(gla_g.reshape(Bsz, T, GLA_HEADS, GLA_DV).astype(f32))
    o_gla = o.reshape(Bsz, T, GLA_V).astype(h.dtype)

    xbc, conv_ssd_new = _causal_conv(ssd_xbc, conv_ssd, p['ssd_conv_w'], p['ssd_conv_b'])
    xs, Bm, Cm = jnp.split(xbc.astype(f32), [SSD_INNER, SSD_INNER + SSD_BC], axis=-1)
    xs = xs.reshape(Bsz, T, SSD_HEADS, SSD_HEADDIM)
    rep = SSD_HEADS // SSD_GROUPS
    Bm = jnp.repeat(Bm.reshape(Bsz, T, SSD_GROUPS, SSD_DSTATE), rep, axis=2)
    Cm = jnp.repeat(Cm.reshape(Bsz, T, SSD_GROUPS, SSD_DSTATE), rep, axis=2)
    dt = jax.nn.softplus(ssd_dt.astype(f32) + p['ssd_dt_bias'].astype(f32))
    a = -jnp.exp(p['ssd_A_log'].astype(f32)) * dt
    y, s_ssd_new = _ssd(xs, dt, a, Bm, Cm, s_ssd.astype(f32))
    y = y + p['ssd_D'].astype(f32)[:, None] * xs
    y = y.reshape(Bsz, T, SSD_INNER) * jax.nn.silu(ssd_z.astype(f32))
    y = _rmsnorm(y.reshape(Bsz, T, SSD_GROUPS, SSD_INNER // SSD_GROUPS),
                 p['ssd_norm'].reshape(SSD_GROUPS, SSD_INNER // SSD_GROUPS))
    o_ssd = y.reshape(Bsz, T, SSD_INNER).astype(h.dtype)

    gates = jax.nn.sigmoid(gate_logits.reshape(Bsz, T, N_BRANCHES, D_MODEL) + p['b_merge'])
    merged = (gates[:, :, 0] * (o_gdn @ p['w_br_gdn'])
              + gates[:, :, 1] * (o_gla @ p['w_br_gla'])
              + gates[:, :, 2] * (o_ssd @ p['w_br_ssd']))
    out = merged @ p['w_out']
    return out, (conv_gdn_new, s_gdn_new, s_gla_new, conv_ssd_new, s_ssd_new)


def _layer(x, state, p):
    f = _swiglu(_rmsnorm(x, p['ffn1_norm_pre']), p['ffn1_w_gate'], p['ffn1_w_up'], p['ffn1_w_down'])
    x = x + (0.5 * _rmsnorm(f, p['ffn1_norm_post'])).astype(x.dtype)
    m, new_state = _token_mixer(_rmsnorm(x, p['mix_norm_pre']), *state, p)
    x = x + _rmsnorm(m, p['mix_norm_post']).astype(x.dtype)
    f = _swiglu(_rmsnorm(x, p['ffn2_norm_pre']), p['ffn2_w_gate'], p['ffn2_w_up'], p['ffn2_w_down'])
    x = x + (0.5 * _rmsnorm(f, p['ffn2_norm_post'])).astype(x.dtype)
    return x, new_state


def setup_inputs(seed: int = 0) -> dict:
    key = jax.random.key(seed)
    ks = iter(jax.random.split(key, 64))
    f32 = jnp.float32

    def nrm(shape, scale):
        return scale * jax.random.normal(next(ks), shape, f32)

    def gain(shape):
        return 1.0 + nrm(shape, 0.02)

    def a_log(h):
        return jnp.log(jax.random.uniform(next(ks), (DEPTH, h), f32, 1.0, 16.0))

    def dt_bias(h):
        dt = jnp.exp(jax.random.uniform(next(ks), (DEPTH, h), f32, float(np.log(1e-3)), float(np.log(1e-1))))
        return jnp.log(jnp.expm1(dt))

    inp = {
        'x_prompt': nrm((BATCH, SEQ, D_MODEL), 1.0),
        'x_sample': nrm((DEC_BATCH, DEC_SEQ, D_MODEL), 1.0),
        'state_gdn_conv': nrm((DEPTH, DEC_BATCH, CONV_WIDTH - 1, GDN_CONV_DIM), 1.0),
        'state_gdn': nrm((DEPTH, DEC_BATCH, GDN_HEADS, GDN_DK, GDN_DV), 0.3),
        'state_gla': nrm((DEPTH, DEC_BATCH, GLA_HEADS, GLA_DK, GLA_DV), 1.0),
        'state_ssd_conv': nrm((DEPTH, DEC_BATCH, CONV_WIDTH - 1, SSD_CONV_DIM), 1.0),
        'state_ssd': nrm((DEPTH, DEC_BATCH, SSD_HEADS, SSD_HEADDIM, SSD_DSTATE), 0.3),
        'ffn1_norm_pre': gain((DEPTH, D_MODEL)),
        'ffn1_norm_post': gain((DEPTH, D_MODEL)),
        'ffn1_w_gate': nrm((DEPTH, D_MODEL, D_FF), D_MODEL ** -0.5),
        'ffn1_w_up': nrm((DEPTH, D_MODEL, D_FF), D_MODEL ** -0.5),
        'ffn1_w_down': nrm((DEPTH, D_FF, D_MODEL), D_FF ** -0.5),
        'mix_norm_pre': gain((DEPTH, D_MODEL)),
        'mix_norm_post': gain((DEPTH, D_MODEL)),
        'w_in': nrm((DEPTH, D_MODEL, D_IN), D_MODEL ** -0.5),
        'b_merge': nrm((DEPTH, N_BRANCHES, D_MODEL), 0.02),
        'gdn_conv_w': nrm((DEPTH, GDN_CONV_DIM, CONV_WIDTH), CONV_WIDTH ** -0.5),
        'gdn_A_log': a_log(GDN_HEADS),
        'gdn_dt_bias': dt_bias(GDN_HEADS),
        'gdn_norm': gain((DEPTH, GDN_DV)),
        'gla_gate_up': nrm((DEPTH, GLA_GATE_RANK, GLA_QK), GLA_GATE_RANK ** -0.5),
        'gla_gate_bias': nrm((DEPTH, GLA_QK), 0.1),
        'gla_norm': gain((DEPTH, GLA_DV)),
        'ssd_conv_w': nrm((DEPTH, SSD_CONV_DIM, CONV_WIDTH), CONV_WIDTH ** -0.5),
        'ssd_conv_b': nrm((DEPTH, SSD_CONV_DIM), 0.02),
        'ssd_A_log': a_log(SSD_HEADS),
        'ssd_dt_bias': dt_bias(SSD_HEADS),
        'ssd_D': gain((DEPTH, SSD_HEADS)),
        'ssd_norm': gain((DEPTH, SSD_INNER)),
        'w_br_gdn': nrm((DEPTH, GDN_V, D_MODEL), GDN_V ** -0.5),
        'w_br_gla': nrm((DEPTH, GLA_V, D_MODEL), GLA_V ** -0.5),
        'w_br_ssd': nrm((DEPTH, SSD_INNER, D_MODEL), SSD_INNER ** -0.5),
        'w_out': nrm((DEPTH, D_MODEL, D_MODEL), D_MODEL ** -0.5),
        'ffn2_norm_pre': gain((DEPTH, D_MODEL)),
        'ffn2_norm_post': gain((DEPTH, D_MODEL)),
        'ffn2_w_gate': nrm((DEPTH, D_MODEL, D_FF), D_MODEL ** -0.5),
        'ffn2_w_up': nrm((DEPTH, D_MODEL, D_FF), D_MODEL ** -0.5),
        'ffn2_w_down': nrm((DEPTH, D_FF, D_MODEL), D_FF ** -0.5),
    }
    return inp


def reference(x_prompt, x_sample, state_gdn_conv, state_gdn, state_gla, state_ssd_conv, state_ssd,
              ffn1_norm_pre, ffn1_norm_post, ffn1_w_gate, ffn1_w_up, ffn1_w_down,
              mix_norm_pre, mix_norm_post, w_in, b_merge,
              gdn_conv_w, gdn_A_log, gdn_dt_bias, gdn_norm,
              gla_gate_up, gla_gate_bias, gla_norm,
              ssd_conv_w, ssd_conv_b, ssd_A_log, ssd_dt_bias, ssd_D, ssd_norm,
              w_br_gdn, w_br_gla, w_br_ssd, w_out,
              ffn2_norm_pre, ffn2_norm_post, ffn2_w_gate, ffn2_w_up, ffn2_w_down):
    params = dict(
        ffn1_norm_pre=ffn1_norm_pre, ffn1_norm_post=ffn1_norm_post, ffn1_w_gate=ffn1_w_gate,
        ffn1_w_up=ffn1_w_up, ffn1_w_down=ffn1_w_down,
        mix_norm_pre=mix_norm_pre, mix_norm_post=mix_norm_post, w_in=w_in, b_merge=b_merge,
        gdn_conv_w=gdn_conv_w, gdn_A_log=gdn_A_log, gdn_dt_bias=gdn_dt_bias, gdn_norm=gdn_norm,
        gla_gate_up=gla_gate_up, gla_gate_bias=gla_gate_bias, gla_norm=gla_norm,
        ssd_conv_w=ssd_conv_w, ssd_conv_b=ssd_conv_b, ssd_A_log=ssd_A_log, ssd_dt_bias=ssd_dt_bias,
        ssd_D=ssd_D, ssd_norm=ssd_norm,
        w_br_gdn=w_br_gdn, w_br_gla=w_br_gla, w_br_ssd=w_br_ssd, w_out=w_out,
        ffn2_norm_pre=ffn2_norm_pre, ffn2_norm_post=ffn2_norm_post, ffn2_w_gate=ffn2_w_gate,
        ffn2_w_up=ffn2_w_up, ffn2_w_down=ffn2_w_down)
    f32 = jnp.float32
    bp = x_prompt.shape[0]
    prompt_init = (jnp.zeros((bp, CONV_WIDTH - 1, GDN_CONV_DIM), x_prompt.dtype),
                   jnp.zeros((bp, GDN_HEADS, GDN_DK, GDN_DV), f32),
                   jnp.zeros((bp, GLA_HEADS, GLA_DK, GLA_DV), f32),
                   jnp.zeros((bp, CONV_WIDTH - 1, SSD_CONV_DIM), x_prompt.dtype),
                   jnp.zeros((bp, SSD_HEADS, SSD_HEADDIM, SSD_DSTATE), f32))
    y_prompt, y_sample = x_prompt, x_sample
    new_p, new_s = [], []
    for l in range(DEPTH):
        p = {name: arr[l] for name, arr in params.items()}
        y_prompt, st_p = _layer(y_prompt, prompt_init, p)
        y_sample, st_s = _layer(
            y_sample,
            (state_gdn_conv[l], state_gdn[l], state_gla[l], state_ssd_conv[l], state_ssd[l]), p)
        new_p.append(st_p)
        new_s.append(st_s)
    p_gdn_conv, p_gdn, p_gla, p_ssd_conv, p_ssd = (jnp.stack(a) for a in zip(*new_p))
    s_gdn_conv, s_gdn, s_gla, s_ssd_conv, s_ssd = (jnp.stack(a) for a in zip(*new_s))
    return (y_prompt, y_sample,
            p_gdn_conv, p_gdn, p_gla, p_ssd_conv, p_ssd,
            s_gdn_conv, s_gdn, s_gla, s_ssd_conv, s_ssd)
```

```python
import functools

import jax
import jax.numpy as jnp
from jax import lax
from jax.experimental import pallas as pl
from jax.experimental.pallas import tpu as pltpu

F32 = jnp.float32
BF16 = jnp.bfloat16

D_MODEL = 1024
DEPTH = 2
D_FF = 2816
FF_CHUNK = 256
N_FF_CHUNKS = D_FF // FF_CHUNK
NORM_EPS = 1e-6
CONV_WIDTH = 4
CHUNK = 64
SUBLANES = 8

GDN_HEADS, GDN_DK, GDN_DV = 4, 128, 128
GLA_HEADS, GLA_DK, GLA_DV = 4, 64, 128
GLA_RANK = 16
GLA_GATE_NORMALIZER = 16.0
GLA_SUB = 16
SSD_HEADS, SSD_P, SSD_N, SSD_GROUPS = 8, 64, 128, 2

GDN_QK = GDN_HEADS * GDN_DK
GDN_V = GDN_HEADS * GDN_DV
GDN_CONV = 2 * GDN_QK + GDN_V
GLA_QK = GLA_HEADS * GLA_DK
GLA_V = GLA_HEADS * GLA_DV
SSD_INNER = SSD_HEADS * SSD_P
SSD_BC = SSD_GROUPS * SSD_N
SSD_CONV = SSD_INNER + 2 * SSD_BC
N_GATES = 3 * D_MODEL
SMALL_W = 128
SM_GDN_A, SM_GDN_B, SM_GLA_R, SM_SSD_DT = 0, 4, 8, 24

PROJ_GROUPS = (("gdn_qkv", GDN_CONV), ("gdn_z", GDN_V), ("gla_qk", 2 * GLA_QK), ("gla_v", GLA_V),
               ("gla_g", GLA_V), ("ssd_z", SSD_INNER), ("ssd_xbc", SSD_CONV), ("gates", N_GATES),
               ("small", SMALL_W))
D_PROJ = sum(w for _, w in PROJ_GROUPS)

VMEM_LIMIT = 56 * 1024 * 1024


def _cparams(n_axes):
    return pltpu.CompilerParams(dimension_semantics=("arbitrary",) * n_axes,
                                vmem_limit_bytes=VMEM_LIMIT)


def _mm(a, b):
    return jnp.dot(a.astype(BF16), b.astype(BF16), preferred_element_type=F32)


def _mm_nt(a, b):
    return lax.dot_general(a.astype(BF16), b.astype(BF16), (((1,), (1,)), ((), ())),
                           preferred_element_type=F32)


def _mm_tn(a, b):
    return lax.dot_general(a, b, (((0,), (0,)), ((), ())), preferred_element_type=F32)


def _mm_hi(a, b):
    return jnp.dot(a, b, precision=lax.Precision.HIGHEST, preferred_element_type=F32)


def _rms(x):
    return x * lax.rsqrt(jnp.mean(x * x, axis=-1, keepdims=True) + NORM_EPS)


def _l2n(x):
    return x * lax.rsqrt(jnp.sum(x * x, axis=-1, keepdims=True) + NORM_EPS)


def _silu(x):
    return x * jax.nn.sigmoid(x)


def _softplus(x):
    return jnp.maximum(x, 0.0) + jnp.log1p(jnp.exp(-jnp.abs(x)))


def _log_sigmoid(x):
    return jnp.minimum(x, 0.0) - jnp.log1p(jnp.exp(-jnp.abs(x)))


def _iota2(n, m, dim):
    return lax.broadcasted_iota(jnp.int32, (n, m), dim)


def _masked_decay(diff, mask):
    return jnp.where(mask, jnp.exp(jnp.where(mask, diff, 0.0)), 0.0)


def _rows_as_lanes(x, n_rows):
    if n_rows < 128:
        x = jnp.concatenate([x, jnp.zeros((128 - n_rows, 128), F32)], axis=0)
    return x.T


def _causal_conv(xp_ref, x_ref, hist_ref, cw_ref, first, n_rows, carry):
    @pl.when(first)
    def _():
        xp_ref[0:SUBLANES, :] = hist_ref[...]
    xp_ref[SUBLANES:SUBLANES + n_rows, :] = x_ref[...]
    base = SUBLANES - (CONV_WIDTH - 1)
    y = xp_ref[base:base + n_rows, :] * cw_ref[0:1, :]
    for i in range(1, CONV_WIDTH):
        y = y + xp_ref[base + i:base + i + n_rows, :] * cw_ref[i:i + 1, :]
    if carry:
        xp_ref[0:SUBLANES, :] = xp_ref[n_rows:n_rows + SUBLANES, :]
    return y


def _unit_lower_solve(a, rhs, n, n_valid):
    if n == CHUNK:
        r, c = _iota2(n, n, 0), _iota2(n, n, 1)
        eye = (r == c).astype(F32)
        same_block = lax.shift_right_logical(r, 4) == lax.shift_right_logical(c, 4)
        d = jnp.where(same_block, a, 0.0)
        off = a - d
        d2 = _mm_hi(d, d)
        d4 = _mm_hi(d2, d2)
        d8 = _mm_hi(d4, d4)
        t = eye - d
        t = t + _mm_hi(t, d2)
        t = t + _mm_hi(t, d4)
        t = t + _mm_hi(t, d8)
        b = _mm_hi(t, off)
        m = eye - b
        m = m + _mm_hi(m, _mm_hi(b, b))
        return _mm_hi(_mm_hi(m, t), rhs)
    rows = [rhs[0:1]]
    for i in range(1, n_valid):
        x = rhs[i:i + 1]
        for j in range(i):
            x = x - a[i:i + 1, j:j + 1] * rows[j]
        rows.append(x)
    if n_valid < n:
        rows.append(rhs[n_valid:n])
    return jnp.concatenate(rows, axis=0)


def _ffn_kernel(x_ref, npre_ref, wg_ref, wu_ref, wd_ref, npost_ref, o_ref, h_ref, acc_ref):
    x = x_ref[...]
    h_ref[...] = (_rms(x) * npre_ref[...]).astype(BF16)
    acc_ref[...] = jnp.zeros_like(acc_ref)

    def body(c, carry):
        h = h_ref[...]
        g = jnp.dot(h, wg_ref[c], preferred_element_type=F32)
        u = jnp.dot(h, wu_ref[c], preferred_element_type=F32)
        acc_ref[...] += jnp.dot((_silu(g) * u).astype(BF16), wd_ref[c], preferred_element_type=F32)
        return carry

    lax.fori_loop(0, N_FF_CHUNKS, body, 0)
    o_ref[...] = x + 0.5 * (_rms(acc_ref[...]) * npost_ref[...])


def _ffn(x, npre, wg, wu, wd, npost, tm):
    n = x.shape[0]
    tm = min(tm, n)
    const3 = lambda i: (0, 0, 0)
    const2 = lambda i: (0, 0)
    return pl.pallas_call(
        _ffn_kernel,
        grid=(n // tm,),
        in_specs=[pl.BlockSpec((tm, D_MODEL), lambda i: (i, 0)),
                  pl.BlockSpec((1, D_MODEL), const2),
                  pl.BlockSpec((N_FF_CHUNKS, D_MODEL, FF_CHUNK), const3),
                  pl.BlockSpec((N_FF_CHUNKS, D_MODEL, FF_CHUNK), const3),
                  pl.BlockSpec((N_FF_CHUNKS, FF_CHUNK, D_MODEL), const3),
                  pl.BlockSpec((1, D_MODEL), const2)],
        out_specs=pl.BlockSpec((tm, D_MODEL), lambda i: (i, 0)),
        out_shape=jax.ShapeDtypeStruct((n, D_MODEL), F32),
        scratch_shapes=[pltpu.VMEM((tm, D_MODEL), BF16), pltpu.VMEM((tm, D_MODEL), F32)],
        compiler_params=_cparams(1),
    )(x, npre, wg, wu, wd, npost)


def _inproj_kernel(x_ref, n_ref, w_ref, *o_refs):
    h = (_rms(x_ref[...]) * n_ref[...]).astype(BF16)
    col = 0
    for (_, width), o_ref in zip(PROJ_GROUPS, o_refs):
        step = min(width, 512)
        for s in range(0, width, step):
            o_ref[:, s:s + step] = jnp.dot(h, w_ref[:, col + s:col + s + step],
                                           preferred_element_type=F32)
        col += width


def _inproj(x, norm, w, tm):
    n = x.shape[0]
    tm = min(tm, n)
    return pl.pallas_call(
        _inproj_kernel,
        grid=(n // tm,),
        in_specs=[pl.BlockSpec((tm, D_MODEL), lambda i: (i, 0)),
                  pl.BlockSpec((1, D_MODEL), lambda i: (0, 0)),
                  pl.BlockSpec((D_MODEL, D_PROJ), lambda i: (0, 0))],
        out_specs=[pl.BlockSpec((tm, wd), lambda i: (i, 0)) for _, wd in PROJ_GROUPS],
        out_shape=[jax.ShapeDtypeStruct((n, wd), F32) for _, wd in PROJ_GROUPS],
        compiler_params=_cparams(1),
    )(x, norm, w)


def _merge_kernel(x_ref, og_ref, ol_ref, os_ref, gl_ref, bm_ref, wg_ref, wl_ref, ws_ref, wo_ref,
                  npost_ref, o_ref):
    merged = None
    for i, (b_ref, w_ref) in enumerate(((og_ref, wg_ref), (ol_ref, wl_ref), (os_ref, ws_ref))):
        gate = jax.nn.sigmoid(gl_ref[:, i * D_MODEL:(i + 1) * D_MODEL] + bm_ref[i:i + 1, :])
        term = gate * _mm(b_ref[...], w_ref[...])
        merged = term if merged is None else merged + term
    out = _mm(merged, wo_ref[...])
    o_ref[...] = x_ref[...] + _rms(out) * npost_ref[...]


def _merge(x, o_gdn, o_gla, o_ssd, gates, b_merge, w_gdn, w_gla, w_ssd, w_out, npost, tm):
    n = x.shape[0]
    tm = min(tm, n)
    row = lambda w: pl.BlockSpec((tm, w), lambda i: (i, 0))
    full = lambda a: pl.BlockSpec(a.shape, lambda i: (0, 0))
    return pl.pallas_call(
        _merge_kernel,
        grid=(n // tm,),
        in_specs=[row(D_MODEL), row(GDN_V), row(GLA_V), row(SSD_INNER), row(N_GATES),
                  full(b_merge), full(w_gdn), full(w_gla), full(w_ssd), full(w_out), full(npost)],
        out_specs=row(D_MODEL),
        out_shape=jax.ShapeDtypeStruct((n, D_MODEL), F32),
        compiler_params=_cparams(1),
    )(x, o_gdn, o_gla, o_ssd, gates, b_merge, w_gdn, w_gla, w_ssd, w_out, npost)


def _gdn_kernel(qkv_ref, z_ref, sm_ref, hist_ref, s0_ref, cw_ref, hp_ref, nrm_ref,
                o_ref, s_ref, xp_ref, *, n, n_valid, carry):
    first = pl.program_id(1) == 0

    @pl.when(first)
    def _():
        s_ref[...] = s0_ref[...]

    qkv = _silu(_causal_conv(xp_ref, qkv_ref, hist_ref, cw_ref, first, n, carry))
    sm = sm_ref[...]
    g_all = -jnp.exp(hp_ref[0:1, :]) * _softplus(sm + hp_ref[1:2, :])
    beta_all = jax.nn.sigmoid(sm)
    valid = _iota2(n, 1, 0) < n_valid
    if n_valid < n:
        g_all = jnp.where(valid, g_all, 0.0)
        beta_all = jnp.where(valid, beta_all, 0.0)
    r, c = _iota2(n, n, 0), _iota2(n, n, 1)
    causal, strict = r >= c, r > c
    gcum = _mm_hi(causal.astype(F32), g_all)
    gcum_t = _rows_as_lanes(gcum, n)

    for h in range(GDN_HEADS):
        gc = gcum[:, SM_GDN_A + h:SM_GDN_A + h + 1]
        gr = gcum_t[SM_GDN_A + h:SM_GDN_A + h + 1, 0:n]
        dec = _masked_decay(gc - gr, causal)
        eg = jnp.exp(gc)
        beta = beta_all[:, SM_GDN_B + h:SM_GDN_B + h + 1]
        q = _l2n(qkv[:, h * GDN_DK:(h + 1) * GDN_DK]) * (GDN_DK ** -0.5)
        k = _l2n(qkv[:, GDN_QK + h * GDN_DK:GDN_QK + (h + 1) * GDN_DK])
        v = qkv[:, 2 * GDN_QK + h * GDN_DV:2 * GDN_QK + (h + 1) * GDN_DV]
        if n_valid < n:
            q, k, v = (jnp.where(valid, t, 0.0) for t in (q, k, v))
        a = jnp.where(strict, beta * _mm_nt(k, k) * dec, 0.0)
        rhs = jnp.concatenate([beta * v, (beta * eg) * k], axis=-1)
        sol = _unit_lower_solve(a, rhs, n, n_valid)
        s = s_ref[h]
        u = sol[:, :GDN_DV] - _mm(sol[:, GDN_DV:], s)
        o = _mm(q * eg, s) + _mm(_mm_nt(q, k) * dec, u)
        g_last = gc[n - 1:n, :]
        s_ref[h] = jnp.exp(g_last) * s + _mm_tn(k * jnp.exp(g_last - gc), u)
        zh = z_ref[:, h * GDN_DV:(h + 1) * GDN_DV]
        o_ref[:, h * GDN_DV:(h + 1) * GDN_DV] = (_rms(o) * nrm_ref[...]) * _silu(zh)


def _gla_kernel(qk_ref, v_ref, g_ref, sm_ref, s0_ref, wup_ref, gb_ref, nrm_ref,
                o_ref, s_ref, *, n, n_valid):
    @pl.when(pl.program_id(1) == 0)
    def _():
        s_ref[...] = s0_ref[...]

    gk = _log_sigmoid(_mm(sm_ref[...], wup_ref[...]) + gb_ref[...]) * (1.0 / GLA_GATE_NORMALIZER)
    q = qk_ref[:, :GLA_QK] * (GLA_DK ** -0.5)
    k = qk_ref[:, GLA_QK:]
    v = v_ref[...]
    rows = _iota2(n, 1, 0)
    if n_valid < n:
        valid = rows < n_valid
        gk = jnp.where(valid, gk, 0.0)
        q, k, v = (jnp.where(valid, t, 0.0) for t in (q, k, v))
    r, c = _iota2(n, n, 0), _iota2(n, n, 1)
    causal = r >= c
    bc = _mm_hi(causal.astype(F32), gk)

    sub = min(GLA_SUB, n)
    att_rows = [[] for _ in range(GLA_HEADS)]
    for blk in range(n // sub):
        lo, hi = blk * sub, (blk + 1) * sub
        ref_row = bc[lo - 1:lo, :] if blk > 0 else jnp.zeros((1, GLA_QK), F32)
        qa = q[lo:hi] * jnp.exp(bc[lo:hi] - ref_row)
        seen = rows < hi
        ka = jnp.where(seen, k * jnp.exp(jnp.where(seen, ref_row - bc, 0.0)), 0.0)
        for h in range(GLA_HEADS):
            hs = slice(h * GLA_DK, (h + 1) * GLA_DK)
            att_rows[h].append(_mm_nt(qa[:, hs], ka[:, hs]))

    b_last = bc[n - 1:n, :]
    q_in = q * jnp.exp(bc)
    k_out = k * jnp.exp(b_last - bc)
    e_last = jnp.exp(b_last)
    for h in range(GLA_HEADS):
        hs = slice(h * GLA_DK, (h + 1) * GLA_DK)
        vs = slice(h * GLA_DV, (h + 1) * GLA_DV)
        att = att_rows[h][0] if len(att_rows[h]) == 1 else jnp.concatenate(att_rows[h], axis=0)
        att = jnp.where(causal, att, 0.0)
        s = s_ref[h]
        o = _mm(q_in[:, hs], s) + _mm(att, v[:, vs])
        pair = (h // 2) * 128
        dcol = jnp.broadcast_to(e_last[:, pair:pair + 128], (128, 128)).T
        dcol = dcol[(h % 2) * GLA_DK:(h % 2 + 1) * GLA_DK, :]
        s_ref[h] = dcol * s + _mm_tn(k_out[:, hs], v[:, vs])
        gh = g_ref[:, vs]
        o_ref[:, vs] = (_rms(o) * nrm_ref[...]) * _silu(gh)


def _ssd_kernel(z_ref, xbc_ref, sm_ref, hist_ref, s0_ref, cw_ref, cb_ref, hp_ref, nrm_ref,
                o_ref, s_ref, xp_ref, *, n, n_valid, carry):
    first = pl.program_id(1) == 0

    @pl.when(first)
    def _():
        s_ref[...] = s0_ref[...]

    xbc = _silu(_causal_conv(xp_ref, xbc_ref, hist_ref, cw_ref, first, n, carry) + cb_ref[...])
    sm = sm_ref[...]
    dt_all = _softplus(sm + hp_ref[1:2, :])
    valid = _iota2(n, 1, 0) < n_valid
    if n_valid < n:
        dt_all = jnp.where(valid, dt_all, 0.0)
    a_all = -jnp.exp(hp_ref[0:1, :]) * dt_all
    r, c = _iota2(n, n, 0), _iota2(n, n, 1)
    causal = r >= c
    acum = _mm_hi(causal.astype(F32), a_all)
    acum_t = _rows_as_lanes(acum, n)
    heads_per_group = SSD_HEADS // SSD_GROUPS
    group_w = SSD_INNER // SSD_GROUPS

    for grp in range(SSD_GROUPS):
        bm = xbc[:, SSD_INNER + grp * SSD_N:SSD_INNER + (grp + 1) * SSD_N]
        cm = xbc[:, SSD_INNER + SSD_BC + grp * SSD_N:SSD_INNER + SSD_BC + (grp + 1) * SSD_N]
        if n_valid < n:
            bm = jnp.where(valid, bm, 0.0)
        cb = _mm_nt(cm, bm)
        ys = []
        for hh in range(heads_per_group):
            h = grp * heads_per_group + hh
            lane = SM_SSD_DT + h
            ac = acum[:, lane:lane + 1]
            ar = acum_t[lane:lane + 1, 0:n]
            dec = _masked_decay(ac - ar, causal)
            xh = xbc[:, h * SSD_P:(h + 1) * SSD_P]
            xdt = xh * dt_all[:, lane:lane + 1]
            s = s_ref[h]
            y = _mm(cb * dec, xdt) + _mm_nt(cm * jnp.exp(ac), s)
            a_last = ac[n - 1:n, :]
            s_ref[h] = jnp.exp(a_last) * s + _mm_tn(xdt * jnp.exp(a_last - ac), bm)
            y = y + hp_ref[2:3, lane:lane + 1] * xh
            ys.append(y * _silu(z_ref[:, h * SSD_P:(h + 1) * SSD_P]))
        ms = sum(jnp.sum(y * y, axis=-1, keepdims=True) for y in ys) * (1.0 / group_w)
        scale = lax.rsqrt(ms + NORM_EPS)
        for hh, y in enumerate(ys):
            cs = slice(grp * group_w + hh * SSD_P, grp * group_w + (hh + 1) * SSD_P)
            o_ref[:, cs] = (y * scale) * nrm_ref[:, cs]


def _seq_block(n, w):
    return pl.BlockSpec((None, n, w), lambda b, c: (b, c, 0))


def _per_seq(shape):
    nd = len(shape)
    return pl.BlockSpec((None,) + tuple(shape[1:]), lambda b, c: (b,) + (0,) * (nd - 1))


def _param(a):
    nd = a.ndim
    return pl.BlockSpec(a.shape, lambda b, c: (0,) * nd)


def _gdn(qkv, z, sm, hist, s0, cw, hp, nrm, n, n_valid):
    bsz, t, _ = qkv.shape
    nc = t // n
    kern = functools.partial(_gdn_kernel, n=n, n_valid=n_valid, carry=nc > 1)
    return pl.pallas_call(
        kern,
        grid=(bsz, nc),
        in_specs=[_seq_block(n, GDN_CONV), _seq_block(n, GDN_V), _seq_block(n, SMALL_W),
                  _per_seq(hist.shape), _per_seq(s0.shape), _param(cw), _param(hp), _param(nrm)],
        out_specs=[_seq_block(n, GDN_V), _per_seq(s0.shape)],
        out_shape=[jax.ShapeDtypeStruct((bsz, t, GDN_V), F32), jax.ShapeDtypeStruct(s0.shape, F32)],
        scratch_shapes=[pltpu.VMEM((SUBLANES + n, GDN_CONV), F32)],
        compiler_params=_cparams(2),
    )(qkv, z, sm, hist, s0, cw, hp, nrm)


def _gla(qk, v, g, sm, s0, wup, gb, nrm, n, n_valid):
    bsz, t, _ = qk.shape
    kern = functools.partial(_gla_kernel, n=n, n_valid=n_valid)
    return pl.pallas_call(
        kern,
        grid=(bsz, t // n),
        in_specs=[_seq_block(n, 2 * GLA_QK), _seq_block(n, GLA_V), _seq_block(n, GLA_V),
                  _seq_block(n, SMALL_W), _per_seq(s0.shape), _param(wup), _param(gb), _param(nrm)],
        out_specs=[_seq_block(n, GLA_V), _per_seq(s0.shape)],
        out_shape=[jax.ShapeDtypeStruct((bsz, t, GLA_V), F32), jax.ShapeDtypeStruct(s0.shape, F32)],
        compiler_params=_cparams(2),
    )(qk, v, g, sm, s0, wup, gb, nrm)


def _ssd(z, xbc, sm, hist, s0, cw, cb, hp, nrm, n, n_valid):
    bsz, t, _ = xbc.shape
    nc = t // n
    kern = functools.partial(_ssd_kernel, n=n, n_valid=n_valid, carry=nc > 1)
    return pl.pallas_call(
        kern,
        grid=(bsz, nc),
        in_specs=[_seq_block(n, SSD_INNER), _seq_block(n, SSD_CONV), _seq_block(n, SMALL_W),
                  _per_seq(hist.shape), _per_seq(s0.shape), _param(cw), _param(cb), _param(hp),
                  _param(nrm)],
        out_specs=[_seq_block(n, SSD_INNER), _per_seq(s0.shape)],
        out_shape=[jax.ShapeDtypeStruct((bsz, t, SSD_INNER), F32),
                   jax.ShapeDtypeStruct(s0.shape, F32)],
        scratch_shapes=[pltpu.VMEM((SUBLANES + n, SSD_CONV), F32)],
        compiler_params=_cparams(2),
    )(z, xbc, sm, hist, s0, cw, cb, hp, nrm)


def _lane_row(pairs):
    row = jnp.zeros((SMALL_W,), F32)
    for off, vec in pairs:
        row = row.at[off:off + vec.shape[0]].set(vec.astype(F32))
    return row[None, :]


def _prep_layer(p):
    def ffn_w(wg, wu, wd):
        split_cols = lambda w: w.reshape(D_MODEL, N_FF_CHUNKS, FF_CHUNK).transpose(1, 0, 2).astype(BF16)
        return split_cols(wg), split_cols(wu), wd.reshape(N_FF_CHUNKS, FF_CHUNK, D_MODEL).astype(BF16)

    sizes = (GDN_CONV, GDN_V, GDN_HEADS, GDN_HEADS, GLA_QK, GLA_QK, GLA_V, GLA_V, GLA_RANK,
             SSD_INNER, SSD_CONV, SSD_HEADS, N_GATES)
    offs = [0]
    for s in sizes:
        offs.append(offs[-1] + s)
    (gdn_qkv, gdn_z, gdn_a, gdn_b, gla_q, gla_k, gla_v, gla_g, gla_r, ssd_z, ssd_xbc, ssd_dt,
     gates) = (p["w_in"][:, offs[i]:offs[i + 1]] for i in range(len(sizes)))
    small = jnp.concatenate(
        [gdn_a, gdn_b, gla_r, ssd_dt,
         jnp.zeros((D_MODEL, SMALL_W - 2 * GDN_HEADS - GLA_RANK - SSD_HEADS), F32)], axis=1)
    w_in = jnp.concatenate([gdn_qkv, gdn_z, gla_q, gla_k, gla_v, gla_g, ssd_z, ssd_xbc, gates, small],
                           axis=1).astype(BF16)
    wup = jnp.zeros((SMALL_W, GLA_QK), F32).at[SM_GLA_R:SM_GLA_R + GLA_RANK].set(p["gla_gate_up"])
    row = lambda v: v.astype(F32)[None, :]
    pad_rows = lambda rws: jnp.concatenate(rws + [jnp.zeros((SUBLANES - len(rws), SMALL_W), F32)], 0)
    return dict(
        ffn1=(row(p["ffn1_norm_pre"]),) + ffn_w(p["ffn1_w_gate"], p["ffn1_w_up"], p["ffn1_w_down"])
        + (row(p["ffn1_norm_post"]),),
        ffn2=(row(p["ffn2_norm_pre"]),) + ffn_w(p["ffn2_w_gate"], p["ffn2_w_up"], p["ffn2_w_down"])
        + (row(p["ffn2_norm_post"]),),
        mix_norm_pre=row(p["mix_norm_pre"]), mix_norm_post=row(p["mix_norm_post"]), w_in=w_in,
        b_merge=p["b_merge"].astype(F32),
        gdn_cw=p["gdn_conv_w"].T.astype(F32),
        gdn_hp=pad_rows([_lane_row([(SM_GDN_A, p["gdn_A_log"])]),
                         _lane_row([(SM_GDN_A, p["gdn_dt_bias"])])]),
        gdn_norm=row(p["gdn_norm"]),
        gla_wup=wup.astype(BF16), gla_gb=row(p["gla_gate_bias"]), gla_norm=row(p["gla_norm"]),
        ssd_cw=p["ssd_conv_w"].T.astype(F32), ssd_cb=row(p["ssd_conv_b"]),
        ssd_hp=pad_rows([_lane_row([(SM_SSD_DT, p["ssd_A_log"])]),
                         _lane_row([(SM_SSD_DT, p["ssd_dt_bias"])]),
                         _lane_row([(SM_SSD_DT, p["ssd_D"])])]),
        ssd_norm=row(p["ssd_norm"]),
        w_br_gdn=p["w_br_gdn"].astype(BF16), w_br_gla=p["w_br_gla"].astype(BF16),
        w_br_ssd=p["w_br_ssd"].astype(BF16), w_out=p["w_out"].astype(BF16))


def _layer(x, state, w, bsz, t, n, n_valid, tm):
    gdn_hist, s_gdn, s_gla, ssd_hist, s_ssd = state
    x = _ffn(x, *w["ffn1"], tm)
    proj = dict(zip((name for name, _ in PROJ_GROUPS),
                    _inproj(x, w["mix_norm_pre"], w["w_in"], min(tm, 256))))
    seq = lambda a: a.reshape(bsz, t, a.shape[-1])
    sm = seq(proj["small"])
    o_gdn, s_gdn_new = _gdn(seq(proj["gdn_qkv"]), seq(proj["gdn_z"]), sm, gdn_hist, s_gdn,
                            w["gdn_cw"], w["gdn_hp"], w["gdn_norm"], n, n_valid)
    o_gla, s_gla_new = _gla(seq(proj["gla_qk"]), seq(proj["gla_v"]), seq(proj["gla_g"]), sm, s_gla,
                            w["gla_wup"], w["gla_gb"], w["gla_norm"], n, n_valid)
    o_ssd, s_ssd_new = _ssd(seq(proj["ssd_z"]), seq(proj["ssd_xbc"]), sm, ssd_hist, s_ssd,
                            w["ssd_cw"], w["ssd_cb"], w["ssd_hp"], w["ssd_norm"], n, n_valid)
    flat = lambda a: a.reshape(bsz * t, a.shape[-1])
    x = _merge(x, flat(o_gdn), flat(o_gla), flat(o_ssd), proj["gates"], w["b_merge"],
               w["w_br_gdn"], w["w_br_gla"], w["w_br_ssd"], w["w_out"], w["mix_norm_post"], tm)
    x = _ffn(x, *w["ffn2"], tm)
    t_valid = t - (n - n_valid)
    tail = lambda a: seq(a)[:, t_valid - (CONV_WIDTH - 1):t_valid]
    return x, (tail(proj["gdn_qkv"]), s_gdn_new, s_gla_new, tail(proj["ssd_xbc"]), s_ssd_new)


def kernel(x_prompt, x_sample, state_gdn_conv, state_gdn, state_gla, state_ssd_conv, state_ssd,
           ffn1_norm_pre, ffn1_norm_post, ffn1_w_gate, ffn1_w_up, ffn1_w_down,
           mix_norm_pre, mix_norm_post, w_in, b_merge,
           gdn_conv_w, gdn_A_log, gdn_dt_bias, gdn_norm,
           gla_gate_up, gla_gate_bias, gla_norm,
           ssd_conv_w, ssd_conv_b, ssd_A_log, ssd_dt_bias, ssd_D, ssd_norm,
           w_br_gdn, w_br_gla, w_br_ssd, w_out,
           ffn2_norm_pre, ffn2_norm_post, ffn2_w_gate, ffn2_w_up, ffn2_w_down):
    params = dict(
        ffn1_norm_pre=ffn1_norm_pre, ffn1_norm_post=ffn1_norm_post, ffn1_w_gate=ffn1_w_gate,
        ffn1_w_up=ffn1_w_up, ffn1_w_down=ffn1_w_down,
        mix_norm_pre=mix_norm_pre, mix_norm_post=mix_norm_post, w_in=w_in, b_merge=b_merge,
        gdn_conv_w=gdn_conv_w, gdn_A_log=gdn_A_log, gdn_dt_bias=gdn_dt_bias, gdn_norm=gdn_norm,
        gla_gate_up=gla_gate_up, gla_gate_bias=gla_gate_bias, gla_norm=gla_norm,
        ssd_conv_w=ssd_conv_w, ssd_conv_b=ssd_conv_b, ssd_A_log=ssd_A_log, ssd_dt_bias=ssd_dt_bias,
        ssd_D=ssd_D, ssd_norm=ssd_norm,
        w_br_gdn=w_br_gdn, w_br_gla=w_br_gla, w_br_ssd=w_br_ssd, w_out=w_out,
        ffn2_norm_pre=ffn2_norm_pre, ffn2_norm_post=ffn2_norm_post, ffn2_w_gate=ffn2_w_gate,
        ffn2_w_up=ffn2_w_up, ffn2_w_down=ffn2_w_down)
    bp, tp, _ = x_prompt.shape
    bs, ts, _ = x_sample.shape
    ts_pad = SUBLANES
    n_p = min(CHUNK, tp)

    prompt_init = (jnp.zeros((bp, SUBLANES, GDN_CONV), F32),
                   jnp.zeros((bp, GDN_HEADS, GDN_DK, GDN_DV), F32),
                   jnp.zeros((bp, GLA_HEADS, GLA_DK, GLA_DV), F32),
                   jnp.zeros((bp, SUBLANES, SSD_CONV), F32),
                   jnp.zeros((bp, SSD_HEADS, SSD_P, SSD_N), F32))
    hist_rows = lambda s: jnp.pad(s.astype(F32), ((0, 0), (SUBLANES - (CONV_WIDTH - 1), 0), (0, 0)))

    y_p = x_prompt.reshape(bp * tp, D_MODEL)
    y_s = jnp.pad(x_sample, ((0, 0), (0, ts_pad - ts), (0, 0))).reshape(bs * ts_pad, D_MODEL)
    new_p, new_s = [], []
    for l in range(DEPTH):
        w = _prep_layer({name: arr[l] for name, arr in params.items()})
        y_p, st_p = _layer(y_p, prompt_init, w, bp, tp, n_p, n_p, 512)
        sample_state = (hist_rows(state_gdn_conv[l]), state_gdn[l].astype(F32),
                        state_gla[l].astype(F32), hist_rows(state_ssd_conv[l]),
                        state_ssd[l].astype(F32))
        y_s, st_s = _layer(y_s, sample_state, w, bs, ts_pad, ts_pad, ts, 512)
        new_p.append(st_p)
        new_s.append(st_s)
    p_out = tuple(jnp.stack(a) for a in zip(*new_p))
    s_out = tuple(jnp.stack(a) for a in zip(*new_s))
    y_prompt = y_p.reshape(bp, tp, D_MODEL)
    y_sample = y_s.reshape(bs, ts_pad, D_MODEL)[:, :ts]
    return (y_prompt, y_sample) + p_out + s_out
```

```python
import functools

import jax
import jax.numpy as jnp
from jax import lax
from jax.experimental import pallas as pl
from jax.experimental.pallas import tpu as pltpu

F32 = jnp.float32
BF16 = jnp.bfloat16

D_MODEL = 1024
DEPTH = 2
D_FF = 2816
FF_CHUNK = 256
N_FF_CHUNKS = D_FF // FF_CHUNK
NORM_EPS = 1e-6
CONV_WIDTH = 4
CHUNK = 64
SUBLANES = 8

GDN_HEADS, GDN_DK, GDN_DV = 4, 128, 128
GLA_HEADS, GLA_DK, GLA_DV = 4, 64, 128
GLA_RANK = 16
GLA_GATE_NORMALIZER = 16.0
GLA_SUB = 16
SSD_HEADS, SSD_P, SSD_N, SSD_GROUPS = 8, 64, 128, 2

GDN_QK = GDN_HEADS * GDN_DK
GDN_V = GDN_HEADS * GDN_DV
GDN_CONV = 2 * GDN_QK + GDN_V
GLA_QK = GLA_HEADS * GLA_DK
GLA_V = GLA_HEADS * GLA_DV
SSD_INNER = SSD_HEADS * SSD_P
SSD_BC = SSD_GROUPS * SSD_N
SSD_CONV = SSD_INNER + 2 * SSD_BC
N_GATES = 3 * D_MODEL
SMALL_W = 128
SM_GDN_A, SM_GDN_B, SM_GLA_R, SM_SSD_DT = 0, 4, 8, 24

PROJ_GROUPS = (("gdn_qkv", GDN_CONV), ("gdn_z", GDN_V), ("gla_qk", 2 * GLA_QK), ("gla_v", GLA_V),
               ("gla_g", GLA_V), ("ssd_z", SSD_INNER), ("ssd_xbc", SSD_CONV), ("gates", N_GATES),
               ("small", SMALL_W))
D_PROJ = sum(w for _, w in PROJ_GROUPS)

VMEM_LIMIT = 56 * 1024 * 1024


def _cparams(n_axes):
    return pltpu.CompilerParams(dimension_semantics=("arbitrary",) * n_axes,
                                vmem_limit_bytes=VMEM_LIMIT)


def _mm(a, b):
    return jnp.dot(a.astype(BF16), b.astype(BF16), preferred_element_type=F32)


def _mm_nt(a, b):
    return lax.dot_general(a.astype(BF16), b.astype(BF16), (((1,), (1,)), ((), ())),
                           preferred_element_type=F32)


def _mm_tn(a, b):
    return lax.dot_general(a, b, (((0,), (0,)), ((), ())), preferred_element_type=F32)


def _mm_hi(a, b):
    return jnp.dot(a, b, precision=lax.Precision.HIGHEST, preferred_element_type=F32)


def _rms(x):
    return x * lax.rsqrt(jnp.mean(x * x, axis=-1, keepdims=True) + NORM_EPS)


def _l2n(x):
    return x * lax.rsqrt(jnp.sum(x * x, axis=-1, keepdims=True) + NORM_EPS)


def _silu(x):
    return x * jax.nn.sigmoid(x)


def _softplus(x):
    return jnp.maximum(x, 0.0) + jnp.log1p(jnp.exp(-jnp.abs(x)))


def _log_sigmoid(x):
    return jnp.minimum(x, 0.0) - jnp.log1p(jnp.exp(-jnp.abs(x)))


def _iota2(n, m, dim):
    return lax.broadcasted_iota(jnp.int32, (n, m), dim)


def _masked_decay(diff, mask):
    return jnp.where(mask, jnp.exp(jnp.where(mask, diff, 0.0)), 0.0)


def _rows_as_lanes(x, n_rows):
    if n_rows < 128:
        x = jnp.concatenate([x, jnp.zeros((128 - n_rows, 128), F32)], axis=0)
    return x.T


def _causal_conv(xp_ref, x_ref, hist_ref, cw_ref, first, n_rows, carry):
    @pl.when(first)
    def _():
        xp_ref[0:SUBLANES, :] = hist_ref[...]
    xp_ref[SUBLANES:SUBLANES + n_rows, :] = x_ref[...]
    base = SUBLANES - (CONV_WIDTH - 1)
    y = xp_ref[base:base + n_rows, :] * cw_ref[0:1, :]
    for i in range(1, CONV_WIDTH):
        y = y + xp_ref[base + i:base + i + n_rows, :] * cw_ref[i:i + 1, :]
    if carry:
        xp_ref[0:SUBLANES, :] = xp_ref[n_rows:n_rows + SUBLANES, :]
    return y


def _each(fn, *lists):
    return [fn(*args) for args in zip(*lists)]


def _unit_lower_solve(a, rhs, n, n_valid):
    if n == CHUNK:
        r, c = _iota2(n, n, 0), _iota2(n, n, 1)
        eye = (r == c).astype(F32)
        same_block = lax.shift_right_logical(r, 4) == lax.shift_right_logical(c, 4)
        d = _each(lambda x: jnp.where(same_block, x, 0.0), a)
        off = _each(lambda x, y: x - y, a, d)
        d2 = _each(_mm_hi, d, d)
        t = _each(lambda x: eye - x, d)
        t = _each(lambda x, y: x + _mm_hi(x, y), t, d2)
        d4 = _each(_mm_hi, d2, d2)
        t = _each(lambda x, y: x + _mm_hi(x, y), t, d4)
        d8 = _each(_mm_hi, d4, d4)
        t = _each(lambda x, y: x + _mm_hi(x, y), t, d8)
        b = _each(_mm_hi, t, off)
        y = _each(_mm_hi, t, rhs)
        b2 = _each(_mm_hi, b, b)
        z = _each(lambda bb, yy: yy - _mm_hi(bb, yy), b, y)
        return _each(lambda bb, zz: zz + _mm_hi(bb, zz), b2, z)
    sols = []
    for ah, rh in zip(a, rhs):
        rows = [rh[0:1]]
        for i in range(1, n_valid):
            x = rh[i:i + 1]
            for j in range(i):
                x = x - ah[i:i + 1, j:j + 1] * rows[j]
            rows.append(x)
        if n_valid < n:
            rows.append(rh[n_valid:n])
        sols.append(jnp.concatenate(rows, axis=0))
    return sols


def _ffn_kernel(x_ref, npre_ref, wg_ref, wu_ref, wd_ref, npost_ref, o_ref, h_ref, acc_ref):
    x = x_ref[...]
    h_ref[...] = (_rms(x) * npre_ref[...]).astype(BF16)
    acc_ref[...] = jnp.zeros_like(acc_ref)

    def body(c, carry):
        h = h_ref[...]
        g = jnp.dot(h, wg_ref[c], preferred_element_type=F32)
        u = jnp.dot(h, wu_ref[c], preferred_element_type=F32)
        acc_ref[...] += jnp.dot((_silu(g) * u).astype(BF16), wd_ref[c], preferred_element_type=F32)
        return carry

    lax.fori_loop(0, N_FF_CHUNKS, body, 0)
    o_ref[...] = x + 0.5 * (_rms(acc_ref[...]) * npost_ref[...])


def _ffn(x, npre, wg, wu, wd, npost, tm):
    n = x.shape[0]
    tm = min(tm, n)
    const3 = lambda i: (0, 0, 0)
    const2 = lambda i: (0, 0)
    return pl.pallas_call(
        _ffn_kernel,
        grid=(n // tm,),
        in_specs=[pl.BlockSpec((tm, D_MODEL), lambda i: (i, 0)),
                  pl.BlockSpec((1, D_MODEL), const2),
                  pl.BlockSpec((N_FF_CHUNKS, D_MODEL, FF_CHUNK), const3),
                  pl.BlockSpec((N_FF_CHUNKS, D_MODEL, FF_CHUNK), const3),
                  pl.BlockSpec((N_FF_CHUNKS, FF_CHUNK, D_MODEL), const3),
                  pl.BlockSpec((1, D_MODEL), const2)],
        out_specs=pl.BlockSpec((tm, D_MODEL), lambda i: (i, 0)),
        out_shape=jax.ShapeDtypeStruct((n, D_MODEL), F32),
        scratch_shapes=[pltpu.VMEM((tm, D_MODEL), BF16), pltpu.VMEM((tm, D_MODEL), F32)],
        compiler_params=_cparams(1),
        name=f"ffn_{n}",
    )(x, npre, wg, wu, wd, npost)


def _inproj_kernel(x_ref, n_ref, w_ref, *o_refs):
    h = (_rms(x_ref[...]) * n_ref[...]).astype(BF16)
    col = 0
    for (_, width), o_ref in zip(PROJ_GROUPS, o_refs):
        step = min(width, 512)
        for s in range(0, width, step):
            o_ref[:, s:s + step] = jnp.dot(h, w_ref[:, col + s:col + s + step],
                                           preferred_element_type=F32)
        col += width


def _inproj(x, norm, w, tm):
    n = x.shape[0]
    tm = min(tm, n)
    return pl.pallas_call(
        _inproj_kernel,
        grid=(n // tm,),
        in_specs=[pl.BlockSpec((tm, D_MODEL), lambda i: (i, 0)),
                  pl.BlockSpec((1, D_MODEL), lambda i: (0, 0)),
                  pl.BlockSpec((D_MODEL, D_PROJ), lambda i: (0, 0))],
        out_specs=[pl.BlockSpec((tm, wd), lambda i: (i, 0)) for _, wd in PROJ_GROUPS],
        out_shape=[jax.ShapeDtypeStruct((n, wd), F32) for _, wd in PROJ_GROUPS],
        compiler_params=_cparams(1),
        name=f"inproj_{n}",
    )(x, norm, w)


def _merge_kernel(x_ref, og_ref, ol_ref, os_ref, gl_ref, bm_ref, wg_ref, wl_ref, ws_ref, wo_ref,
                  npost_ref, o_ref):
    merged = None
    for i, (b_ref, w_ref) in enumerate(((og_ref, wg_ref), (ol_ref, wl_ref), (os_ref, ws_ref))):
        gate = jax.nn.sigmoid(gl_ref[:, i * D_MODEL:(i + 1) * D_MODEL] + bm_ref[i:i + 1, :])
        term = gate * _mm(b_ref[...], w_ref[...])
        merged = term if merged is None else merged + term
    out = _mm(merged, wo_ref[...])
    o_ref[...] = x_ref[...] + _rms(out) * npost_ref[...]


def _merge(x, o_gdn, o_gla, o_ssd, gates, b_merge, w_gdn, w_gla, w_ssd, w_out, npost, tm):
    n = x.shape[0]
    tm = min(tm, n)
    row = lambda w: pl.BlockSpec((tm, w), lambda i: (i, 0))
    full = lambda a: pl.BlockSpec(a.shape, lambda i: (0, 0))
    return pl.pallas_call(
        _merge_kernel,
        grid=(n // tm,),
        in_specs=[row(D_MODEL), row(GDN_V), row(GLA_V), row(SSD_INNER), row(N_GATES),
                  full(b_merge), full(w_gdn), full(w_gla), full(w_ssd), full(w_out), full(npost)],
        out_specs=row(D_MODEL),
        out_shape=jax.ShapeDtypeStruct((n, D_MODEL), F32),
        compiler_params=_cparams(1),
        name=f"merge_{n}",
    )(x, o_gdn, o_gla, o_ssd, gates, b_merge, w_gdn, w_gla, w_ssd, w_out, npost)


def _gdn_kernel(qkv_ref, z_ref, sm_ref, hist_ref, s0_ref, cw_ref, hp_ref, nrm_ref,
                o_ref, s_ref, xp_ref, *, n, n_valid, carry):
    first = pl.program_id(1) == 0

    @pl.when(first)
    def _():
        s_ref[...] = s0_ref[...]

    qkv = _silu(_causal_conv(xp_ref, qkv_ref, hist_ref, cw_ref, first, n, carry))
    sm = sm_ref[...]
    g_all = -jnp.exp(hp_ref[0:1, :]) * _softplus(sm + hp_ref[1:2, :])
    beta_all = jax.nn.sigmoid(sm)
    valid = _iota2(n, 1, 0) < n_valid
    if n_valid < n:
        g_all = jnp.where(valid, g_all, 0.0)
        beta_all = jnp.where(valid, beta_all, 0.0)
    r, c = _iota2(n, n, 0), _iota2(n, n, 1)
    causal, strict = r >= c, r > c
    gcum = _mm_hi(causal.astype(F32), g_all)
    gcum_t = _rows_as_lanes(gcum, n)

    heads = list(range(GDN_HEADS))
    gc = [gcum[:, SM_GDN_A + h:SM_GDN_A + h + 1] for h in heads]
    gr = [gcum_t[SM_GDN_A + h:SM_GDN_A + h + 1, 0:n] for h in heads]
    beta = [beta_all[:, SM_GDN_B + h:SM_GDN_B + h + 1] for h in heads]
    pad = (lambda t: jnp.where(valid, t, 0.0)) if n_valid < n else (lambda t: t)
    q = [pad(_l2n(qkv[:, h * GDN_DK:(h + 1) * GDN_DK]) * (GDN_DK ** -0.5)) for h in heads]
    k = [pad(_l2n(qkv[:, GDN_QK + h * GDN_DK:GDN_QK + (h + 1) * GDN_DK])) for h in heads]
    v = [pad(qkv[:, 2 * GDN_QK + h * GDN_DV:2 * GDN_QK + (h + 1) * GDN_DV]) for h in heads]
    dec = _each(lambda x, y: _masked_decay(x - y, causal), gc, gr)
    eg = _each(jnp.exp, gc)
    kk = _each(_mm_nt, k, k)
    qk = _each(_mm_nt, q, k)
    a = _each(lambda b, x, d: jnp.where(strict, b * x * d, 0.0), beta, kk, dec)
    rhs = _each(lambda b, e, vv, kx: jnp.concatenate([b * vv, (b * e) * kx], axis=-1), beta, eg, v, k)
    sol = _unit_lower_solve(a, rhs, n, n_valid)
    s = [s_ref[h] for h in heads]
    u = _each(lambda x, ss: x[:, :GDN_DV] - _mm(x[:, GDN_DV:], ss), sol, s)
    o = _each(lambda qq, e, ss, x, d, uu: _mm(qq * e, ss) + _mm(x * d, uu), q, eg, s, qk, dec, u)
    g_last = [x[n - 1:n, :] for x in gc]
    s_new = _each(lambda gl, ss, kx, x, uu: jnp.exp(gl) * ss + _mm_tn(kx * jnp.exp(gl - x), uu),
                  g_last, s, k, gc, u)
    for h in heads:
        s_ref[h] = s_new[h]
        zh = z_ref[:, h * GDN_DV:(h + 1) * GDN_DV]
        o_ref[:, h * GDN_DV:(h + 1) * GDN_DV] = (_rms(o[h]) * nrm_ref[...]) * _silu(zh)


def _gla_kernel(qk_ref, v_ref, g_ref, sm_ref, s0_ref, wup_ref, gb_ref, nrm_ref,
                o_ref, s_ref, *, n, n_valid):
    @pl.when(pl.program_id(1) == 0)
    def _():
        s_ref[...] = s0_ref[...]

    gk = _log_sigmoid(_mm(sm_ref[...], wup_ref[...]) + gb_ref[...]) * (1.0 / GLA_GATE_NORMALIZER)
    q = qk_ref[:, :GLA_QK] * (GLA_DK ** -0.5)
    k = qk_ref[:, GLA_QK:]
    v = v_ref[...]
    rows = _iota2(n, 1, 0)
    if n_valid < n:
        valid = rows < n_valid
        gk = jnp.where(valid, gk, 0.0)
        q, k, v = (jnp.where(valid, t, 0.0) for t in (q, k, v))
    r, c = _iota2(n, n, 0), _iota2(n, n, 1)
    causal = r >= c
    bc = _mm_hi(causal.astype(F32), gk)

    sub = min(GLA_SUB, n)
    att_rows = [[] for _ in range(GLA_HEADS)]
    for blk in range(n // sub):
        lo, hi = blk * sub, (blk + 1) * sub
        ref_row = bc[lo - 1:lo, :] if blk > 0 else jnp.zeros((1, GLA_QK), F32)
        qa = q[lo:hi] * jnp.exp(bc[lo:hi] - ref_row)
        seen = rows < hi
        ka = jnp.where(seen, k * jnp.exp(jnp.where(seen, ref_row - bc, 0.0)), 0.0)
        for h in range(GLA_HEADS):
            hs = slice(h * GLA_DK, (h + 1) * GLA_DK)
            att_rows[h].append(_mm_nt(qa[:, hs], ka[:, hs]))

    b_last = bc[n - 1:n, :]
    q_in = q * jnp.exp(bc)
    k_out = k * jnp.exp(b_last - bc)
    e_last = jnp.exp(b_last)
    heads = list(range(GLA_HEADS))
    hs = [slice(h * GLA_DK, (h + 1) * GLA_DK) for h in heads]
    vs = [slice(h * GLA_DV, (h + 1) * GLA_DV) for h in heads]
    att = [jnp.where(causal, x[0] if len(x) == 1 else jnp.concatenate(x, axis=0), 0.0)
           for x in att_rows]
    s = [s_ref[h] for h in heads]
    o_in = _each(lambda sl, ss: _mm(q_in[:, sl], ss), hs, s)
    o_att = _each(lambda x, sl: _mm(x, v[:, sl]), att, vs)
    upd = _each(lambda sl, vl: _mm_tn(k_out[:, sl], v[:, vl]), hs, vs)
    dpair = [jnp.broadcast_to(e_last[:, p * 128:(p + 1) * 128], (128, 128)).T
             for p in range(GLA_HEADS // 2)]
    for h in heads:
        dcol = dpair[h // 2][(h % 2) * GLA_DK:(h % 2 + 1) * GLA_DK, :]
        s_ref[h] = dcol * s[h] + upd[h]
        o_ref[:, vs[h]] = (_rms(o_in[h] + o_att[h]) * nrm_ref[...]) * _silu(g_ref[:, vs[h]])


def _ssd_kernel(z_ref, xbc_ref, sm_ref, hist_ref, s0_ref, cw_ref, cb_ref, hp_ref, nrm_ref,
                o_ref, s_ref, xp_ref, *, n, n_valid, carry):
    first = pl.program_id(1) == 0

    @pl.when(first)
    def _():
        s_ref[...] = s0_ref[...]

    xbc = _silu(_causal_conv(xp_ref, xbc_ref, hist_ref, cw_ref, first, n, carry) + cb_ref[...])
    sm = sm_ref[...]
    dt_all = _softplus(sm + hp_ref[1:2, :])
    valid = _iota2(n, 1, 0) < n_valid
    if n_valid < n:
        dt_all = jnp.where(valid, dt_all, 0.0)
    a_all = -jnp.exp(hp_ref[0:1, :]) * dt_all
    r, c = _iota2(n, n, 0), _iota2(n, n, 1)
    causal = r >= c
    acum = _mm_hi(causal.astype(F32), a_all)
    acum_t = _rows_as_lanes(acum, n)
    heads_per_group = SSD_HEADS // SSD_GROUPS
    group_w = SSD_INNER // SSD_GROUPS

    groups = list(range(SSD_GROUPS))
    bm = [xbc[:, SSD_INNER + g * SSD_N:SSD_INNER + (g + 1) * SSD_N] for g in groups]
    cm = [xbc[:, SSD_INNER + SSD_BC + g * SSD_N:SSD_INNER + SSD_BC + (g + 1) * SSD_N] for g in groups]
    if n_valid < n:
        bm = [jnp.where(valid, x, 0.0) for x in bm]
    cb = _each(_mm_nt, cm, bm)
    heads = list(range(SSD_HEADS))
    grp = [h // heads_per_group for h in heads]
    lanes = [SM_SSD_DT + h for h in heads]
    ac = [acum[:, ln:ln + 1] for ln in lanes]
    ar = [acum_t[ln:ln + 1, 0:n] for ln in lanes]
    dec = _each(lambda x, y: _masked_decay(x - y, causal), ac, ar)
    xh = [xbc[:, h * SSD_P:(h + 1) * SSD_P] for h in heads]
    xdt = _each(lambda x, ln: x * dt_all[:, ln:ln + 1], xh, lanes)
    s = [s_ref[h] for h in heads]
    y_att = _each(lambda g, d, x: _mm(cb[g] * d, x), grp, dec, xdt)
    y_in = _each(lambda g, a, ss: _mm_nt(cm[g] * jnp.exp(a), ss), grp, ac, s)
    a_last = [a[n - 1:n, :] for a in ac]
    upd = _each(lambda g, x, al, a: _mm_tn(x * jnp.exp(al - a), bm[g]), grp, xdt, a_last, ac)
    ys = []
    for h in heads:
        s_ref[h] = jnp.exp(a_last[h]) * s[h] + upd[h]
        y = y_att[h] + y_in[h] + hp_ref[2:3, lanes[h]:lanes[h] + 1] * xh[h]
        ys.append(y * _silu(z_ref[:, h * SSD_P:(h + 1) * SSD_P]))
    for g in groups:
        mine = [h for h in heads if grp[h] == g]
        ms = sum(jnp.sum(ys[h] * ys[h], axis=-1, keepdims=True) for h in mine) * (1.0 / group_w)
        scale = lax.rsqrt(ms + NORM_EPS)
        for h in mine:
            cs = slice(h * SSD_P, (h + 1) * SSD_P)
            o_ref[:, cs] = (ys[h] * scale) * nrm_ref[:, cs]


def _seq_block(nb, n, w):
    return pl.BlockSpec((None if nb == 1 else nb, n, w), lambda b, c: (b, c, 0))


def _per_seq(nb, shape):
    nd = len(shape)
    return pl.BlockSpec((None if nb == 1 else nb,) + tuple(shape[1:]),
                        lambda b, c: (b,) + (0,) * (nd - 1))


def _param(a):
    nd = a.ndim
    return pl.BlockSpec(a.shape, lambda b, c: (0,) * nd)


def _per_sequence(body, nb, n_seq_in, n_param, n_out):
    def kern(*refs):
        seq_in = refs[:n_seq_in]
        params = refs[n_seq_in:n_seq_in + n_param]
        outs = refs[n_seq_in + n_param:n_seq_in + n_param + n_out]
        scratch = refs[n_seq_in + n_param + n_out:]
        if nb == 1:
            body(*seq_in, *params, *outs, *scratch)
            return

        def step(i, carry):
            body(*(r.at[i] for r in seq_in), *params, *(r.at[i] for r in outs), *scratch)
            return carry

        lax.fori_loop(0, nb, step, 0)
    return kern


def _seqs_per_step(bsz, nc):
    nb = 8
    return nb if nc == 1 and bsz % nb == 0 else 1


def _gdn(qkv, z, sm, hist, s0, cw, hp, nrm, n, n_valid):
    bsz, t, _ = qkv.shape
    nc = t // n
    nb = _seqs_per_step(bsz, nc)
    body = functools.partial(_gdn_kernel, n=n, n_valid=n_valid, carry=nc > 1)
    return pl.pallas_call(
        _per_sequence(body, nb, 5, 3, 2),
        grid=(bsz // nb, nc),
        in_specs=[_seq_block(nb, n, GDN_CONV), _seq_block(nb, n, GDN_V), _seq_block(nb, n, SMALL_W),
                  _per_seq(nb, hist.shape), _per_seq(nb, s0.shape),
                  _param(cw), _param(hp), _param(nrm)],
        out_specs=[_seq_block(nb, n, GDN_V), _per_seq(nb, s0.shape)],
        out_shape=[jax.ShapeDtypeStruct((bsz, t, GDN_V), F32), jax.ShapeDtypeStruct(s0.shape, F32)],
        scratch_shapes=[pltpu.VMEM((SUBLANES + n, GDN_CONV), F32)],
        compiler_params=_cparams(2),
        name=f"gdn_{n}",
    )(qkv, z, sm, hist, s0, cw, hp, nrm)


def _gla(qk, v, g, sm, s0, wup, gb, nrm, n, n_valid):
    bsz, t, _ = qk.shape
    nc = t // n
    nb = _seqs_per_step(bsz, nc)
    body = functools.partial(_gla_kernel, n=n, n_valid=n_valid)
    return pl.pallas_call(
        _per_sequence(body, nb, 5, 3, 2),
        grid=(bsz // nb, nc),
        in_specs=[_seq_block(nb, n, 2 * GLA_QK), _seq_block(nb, n, GLA_V), _seq_block(nb, n, GLA_V),
                  _seq_block(nb, n, SMALL_W), _per_seq(nb, s0.shape),
                  _param(wup), _param(gb), _param(nrm)],
        out_specs=[_seq_block(nb, n, GLA_V), _per_seq(nb, s0.shape)],
        out_shape=[jax.ShapeDtypeStruct((bsz, t, GLA_V), F32), jax.ShapeDtypeStruct(s0.shape, F32)],
        compiler_params=_cparams(2),
        name=f"gla_{n}",
    )(qk, v, g, sm, s0, wup, gb, nrm)


def _ssd(z, xbc, sm, hist, s0, cw, cb, hp, nrm, n, n_valid):
    bsz, t, _ = xbc.shape
    nc = t // n
    nb = _seqs_per_step(bsz, nc)
    body = functools.partial(_ssd_kernel, n=n, n_valid=n_valid, carry=nc > 1)
    return pl.pallas_call(
        _per_sequence(body, nb, 5, 4, 2),
        grid=(bsz // nb, nc),
        in_specs=[_seq_block(nb, n, SSD_INNER), _seq_block(nb, n, SSD_CONV),
                  _seq_block(nb, n, SMALL_W), _per_seq(nb, hist.shape), _per_seq(nb, s0.shape),
                  _param(cw), _param(cb), _param(hp), _param(nrm)],
        out_specs=[_seq_block(nb, n, SSD_INNER), _per_seq(nb, s0.shape)],
        out_shape=[jax.ShapeDtypeStruct((bsz, t, SSD_INNER), F32),
                   jax.ShapeDtypeStruct(s0.shape, F32)],
        scratch_shapes=[pltpu.VMEM((SUBLANES + n, SSD_CONV), F32)],
        compiler_params=_cparams(2),
        name=f"ssd_{n}",
    )(z, xbc, sm, hist, s0, cw, cb, hp, nrm)


def _lane_row(pairs):
    row = jnp.zeros((SMALL_W,), F32)
    for off, vec in pairs:
        row = row.at[off:off + vec.shape[0]].set(vec.astype(F32))
    return row[None, :]


def _prep_layer(p):
    def ffn_w(wg, wu, wd):
        split_cols = lambda w: w.reshape(D_MODEL, N_FF_CHUNKS, FF_CHUNK).transpose(1, 0, 2).astype(BF16)
        return split_cols(wg), split_cols(wu), wd.reshape(N_FF_CHUNKS, FF_CHUNK, D_MODEL).astype(BF16)

    sizes = (GDN_CONV, GDN_V, GDN_HEADS, GDN_HEADS, GLA_QK, GLA_QK, GLA_V, GLA_V, GLA_RANK,
             SSD_INNER, SSD_CONV, SSD_HEADS, N_GATES)
    offs = [0]
    for s in sizes:
        offs.append(offs[-1] + s)
    (gdn_qkv, gdn_z, gdn_a, gdn_b, gla_q, gla_k, gla_v, gla_g, gla_r, ssd_z, ssd_xbc, ssd_dt,
     gates) = (p["w_in"][:, offs[i]:offs[i + 1]] for i in range(len(sizes)))
    small = jnp.concatenate(
        [gdn_a, gdn_b, gla_r, ssd_dt,
         jnp.zeros((D_MODEL, SMALL_W - 2 * GDN_HEADS - GLA_RANK - SSD_HEADS), F32)], axis=1)
    w_in = jnp.concatenate([gdn_qkv, gdn_z, gla_q, gla_k, gla_v, gla_g, ssd_z, ssd_xbc, gates, small],
                           axis=1).astype(BF16)
    wup = jnp.zeros((SMALL_W, GLA_QK), F32).at[SM_GLA_R:SM_GLA_R + GLA_RANK].set(p["gla_gate_up"])
    row = lambda v: v.astype(F32)[None, :]
    pad_rows = lambda rws: jnp.concatenate(rws + [jnp.zeros((SUBLANES - len(rws), SMALL_W), F32)], 0)
    return dict(
        ffn1=(row(p["ffn1_norm_pre"]),) + ffn_w(p["ffn1_w_gate"], p["ffn1_w_up"], p["ffn1_w_down"])
        + (row(p["ffn1_norm_post"]),),
        ffn2=(row(p["ffn2_norm_pre"]),) + ffn_w(p["ffn2_w_gate"], p["ffn2_w_up"], p["ffn2_w_down"])
        + (row(p["ffn2_norm_post"]),),
        mix_norm_pre=row(p["mix_norm_pre"]), mix_norm_post=row(p["mix_norm_post"]), w_in=w_in,
        b_merge=p["b_merge"].astype(F32),
        gdn_cw=p["gdn_conv_w"].T.astype(F32),
        gdn_hp=pad_rows([_lane_row([(SM_GDN_A, p["gdn_A_log"])]),
                         _lane_row([(SM_GDN_A, p["gdn_dt_bias"])])]),
        gdn_norm=row(p["gdn_norm"]),
        gla_wup=wup.astype(BF16), gla_gb=row(p["gla_gate_bias"]), gla_norm=row(p["gla_norm"]),
        ssd_cw=p["ssd_conv_w"].T.astype(F32), ssd_cb=row(p["ssd_conv_b"]),
        ssd_hp=pad_rows([_lane_row([(SM_SSD_DT, p["ssd_A_log"])]),
                         _lane_row([(SM_SSD_DT, p["ssd_dt_bias"])]),
                         _lane_row([(SM_SSD_DT, p["ssd_D"])])]),
        ssd_norm=row(p["ssd_norm"]),
        w_br_gdn=p["w_br_gdn"].astype(BF16), w_br_gla=p["w_br_gla"].astype(BF16),
        w_br_ssd=p["w_br_ssd"].astype(BF16), w_out=p["w_out"].astype(BF16))


def _layer(x, state, w, bsz, t, n, n_valid, tm):
    gdn_hist, s_gdn, s_gla, ssd_hist, s_ssd = state
    x = _ffn(x, *w["ffn1"], tm)
    proj = dict(zip((name for name, _ in PROJ_GROUPS),
                    _inproj(x, w["mix_norm_pre"], w["w_in"], min(tm, 256))))
    seq = lambda a: a.reshape(bsz, t, a.shape[-1])
    sm = seq(proj["small"])
    o_gdn, s_gdn_new = _gdn(seq(proj["gdn_qkv"]), seq(proj["gdn_z"]), sm, gdn_hist, s_gdn,
                            w["gdn_cw"], w["gdn_hp"], w["gdn_norm"], n, n_valid)
    o_gla, s_gla_new = _gla(seq(proj["gla_qk"]), seq(proj["gla_v"]), seq(proj["gla_g"]), sm, s_gla,
                            w["gla_wup"], w["gla_gb"], w["gla_norm"], n, n_valid)
    o_ssd, s_ssd_new = _ssd(seq(proj["ssd_z"]), seq(proj["ssd_xbc"]), sm, ssd_hist, s_ssd,
                            w["ssd_cw"], w["ssd_cb"], w["ssd_hp"], w["ssd_norm"], n, n_valid)
    flat = lambda a: a.reshape(bsz * t, a.shape[-1])
    x = _merge(x, flat(o_gdn), flat(o_gla), flat(o_ssd), proj["gates"], w["b_merge"],
               w["w_br_gdn"], w["w_br_gla"], w["w_br_ssd"], w["w_out"], w["mix_norm_post"], tm)
    x = _ffn(x, *w["ffn2"], tm)
    t_valid = t - (n - n_valid)
    tail = lambda a: seq(a)[:, t_valid - (CONV_WIDTH - 1):t_valid]
    return x, (tail(proj["gdn_qkv"]), s_gdn_new, s_gla_new, tail(proj["ssd_xbc"]), s_ssd_new)


def kernel(x_prompt, x_sample, state_gdn_conv, state_gdn, state_gla, state_ssd_conv, state_ssd,
           ffn1_norm_pre, ffn1_norm_post, ffn1_w_gate, ffn1_w_up, ffn1_w_down,
           mix_norm_pre, mix_norm_post, w_in, b_merge,
           gdn_conv_w, gdn_A_log, gdn_dt_bias, gdn_norm,
           gla_gate_up, gla_gate_bias, gla_norm,
           ssd_conv_w, ssd_conv_b, ssd_A_log, ssd_dt_bias, ssd_D, ssd_norm,
           w_br_gdn, w_br_gla, w_br_ssd, w_out,
           ffn2_norm_pre, ffn2_norm_post, ffn2_w_gate, ffn2_w_up, ffn2_w_down):
    params = dict(
        ffn1_norm_pre=ffn1_norm_pre, ffn1_norm_post=ffn1_norm_post, ffn1_w_gate=ffn1_w_gate,
        ffn1_w_up=ffn1_w_up, ffn1_w_down=ffn1_w_down,
        mix_norm_pre=mix_norm_pre, mix_norm_post=mix_norm_post, w_in=w_in, b_merge=b_merge,
        gdn_conv_w=gdn_conv_w, gdn_A_log=gdn_A_log, gdn_dt_bias=gdn_dt_bias, gdn_norm=gdn_norm,
        gla_gate_up=gla_gate_up, gla_gate_bias=gla_gate_bias, gla_norm=gla_norm,
        ssd_conv_w=ssd_conv_w, ssd_conv_b=ssd_conv_b, ssd_A_log=ssd_A_log, ssd_dt_bias=ssd_dt_bias,
        ssd_D=ssd_D, ssd_norm=ssd_norm,
        w_br_gdn=w_br_gdn, w_br_gla=w_br_gla, w_br_ssd=w_br_ssd, w_out=w_out,
        ffn2_norm_pre=ffn2_norm_pre, ffn2_norm_post=ffn2_norm_post, ffn2_w_gate=ffn2_w_gate,
        ffn2_w_up=ffn2_w_up, ffn2_w_down=ffn2_w_down)
    bp, tp, _ = x_prompt.shape
    bs, ts, _ = x_sample.shape
    ts_pad = SUBLANES
    n_p = min(CHUNK, tp)

    prompt_init = (jnp.zeros((bp, SUBLANES, GDN_CONV), F32),
                   jnp.zeros((bp, GDN_HEADS, GDN_DK, GDN_DV), F32),
                   jnp.zeros((bp, GLA_HEADS, GLA_DK, GLA_DV), F32),
                   jnp.zeros((bp, SUBLANES, SSD_CONV), F32),
                   jnp.zeros((bp, SSD_HEADS, SSD_P, SSD_N), F32))
    hist_rows = lambda s: jnp.pad(s.astype(F32), ((0, 0), (SUBLANES - (CONV_WIDTH - 1), 0), (0, 0)))

    y_p = x_prompt.reshape(bp * tp, D_MODEL)
    y_s = jnp.pad(x_sample, ((0, 0), (0, ts_pad - ts), (0, 0))).reshape(bs * ts_pad, D_MODEL)
    new_p, new_s = [], []
    for l in range(DEPTH):
        w = _prep_layer({name: arr[l] for name, arr in params.items()})
        y_p, st_p = _layer(y_p, prompt_init, w, bp, tp, n_p, n_p, 512)
        sample_state = (hist_rows(state_gdn_conv[l]), state_gdn[l].astype(F32),
                        state_gla[l].astype(F32), hist_rows(state_ssd_conv[l]),
                        state_ssd[l].astype(F32))
        y_s, st_s = _layer(y_s, sample_state, w, bs, ts_pad, ts_pad, ts, 512)
        new_p.append(st_p)
        new_s.append(st_s)
    p_out = tuple(jnp.stack(a) for a in zip(*new_p))
    s_out = tuple(jnp.stack(a) for a in zip(*new_s))
    y_prompt = y_p.reshape(bp, tp, D_MODEL)
    y_sample = y_s.reshape(bs, ts_pad, D_MODEL)[:, :ts]
    return (y_prompt, y_sample) + p_out + s_out
```

```python
import functools

import jax
import jax.numpy as jnp
import numpy as np
from jax import lax
from jax.experimental import pallas as pl
from jax.experimental.pallas import tpu as pltpu

F32 = jnp.float32
BF16 = jnp.bfloat16

D_MODEL = 1024
DEPTH = 2
D_FF = 2816
MXU_WIDTH = 256
NORM_EPS = 1e-6
CONV_WIDTH = 4
CHUNK = 64
SUBLANES = 8

GDN_HEADS, GDN_DK, GDN_DV = 4, 128, 128
GLA_HEADS, GLA_DK, GLA_DV = 4, 64, 128
GLA_RANK = 16
GLA_GATE_NORMALIZER = 16.0
GLA_SUB = 16
SSD_HEADS, SSD_P, SSD_N, SSD_GROUPS = 8, 64, 128, 2

GDN_QK = GDN_HEADS * GDN_DK
GDN_V = GDN_HEADS * GDN_DV
GDN_CONV = 2 * GDN_QK + GDN_V
GLA_QK = GLA_HEADS * GLA_DK
GLA_V = GLA_HEADS * GLA_DV
SSD_INNER = SSD_HEADS * SSD_P
SSD_BC = SSD_GROUPS * SSD_N
SSD_CONV = SSD_INNER + 2 * SSD_BC
N_GATES = 3 * D_MODEL
SMALL_W = 128
SM_GDN_A, SM_GDN_B, SM_GLA_R, SM_SSD_DT = 0, 4, 8, 24

PROJ_GROUPS = (("gdn_qkv", GDN_CONV), ("gdn_z", GDN_V), ("gla_qk", 2 * GLA_QK), ("gla_v", GLA_V),
               ("gla_g", GLA_V), ("ssd_z", SSD_INNER), ("ssd_xbc", SSD_CONV), ("gates", N_GATES),
               ("small", SMALL_W))
D_PROJ = sum(w for _, w in PROJ_GROUPS)

VMEM_LIMIT = 56 * 1024 * 1024
ROW_TILE = 512
INPROJ_ROW_TILE = 256
PROMPT_CHUNKS_PER_STEP = 4
DECODE_SEQS_PER_STEP = 8
DECODE_SEQS_TOGETHER = 4


def _cparams(n_axes):
    return pltpu.CompilerParams(dimension_semantics=("arbitrary",) * n_axes,
                                vmem_limit_bytes=VMEM_LIMIT)


def _mm(a, b):
    return jnp.dot(a.astype(BF16), b.astype(BF16), preferred_element_type=F32)


def _mm_nt(a, b):
    return lax.dot_general(a.astype(BF16), b.astype(BF16), (((1,), (1,)), ((), ())),
                           preferred_element_type=F32)


def _mm_tn(a, b):
    return lax.dot_general(a, b, (((0,), (0,)), ((), ())), preferred_element_type=F32)


def _split2(x):
    hi = x.astype(BF16).astype(F32)
    return hi, (x - hi).astype(BF16).astype(F32)


def _split3_lanes(x):
    hi = x.astype(BF16)
    r = x - hi.astype(F32)
    mid = r.astype(BF16)
    lo = (r - mid.astype(F32)).astype(BF16)
    return jnp.concatenate([hi, mid, lo], axis=1)


def _mm_x3(a2, b2):
    (ah, al), (bh, bl) = a2, b2
    lhs = jnp.concatenate([ah, al, ah], axis=1).astype(BF16)
    rhs = jnp.concatenate([bh, bh, bl], axis=0).astype(BF16)
    return jnp.dot(lhs, rhs, preferred_element_type=F32)


def _cumsum_rows(tri, x):
    w = x.shape[1]
    y = jnp.dot(tri, _split3_lanes(x), preferred_element_type=F32)
    return y[:, :w] + y[:, w:2 * w] + y[:, 2 * w:]


def _lane_spread(x, sel3_ref):
    return jnp.dot(_split3_lanes(x), sel3_ref[...], preferred_element_type=F32)


def _rms(x):
    return x * lax.rsqrt(jnp.mean(x * x, axis=-1, keepdims=True) + NORM_EPS)


def _l2n(x):
    return x * lax.rsqrt(jnp.sum(x * x, axis=-1, keepdims=True) + NORM_EPS)


def _silu(x):
    return x * jax.nn.sigmoid(x)


def _softplus(x):
    return jnp.maximum(x, 0.0) + jnp.log1p(jnp.exp(-jnp.abs(x)))


def _log_sigmoid(x):
    return jnp.minimum(x, 0.0) - jnp.log1p(jnp.exp(-jnp.abs(x)))


def _iota2(n, m, dim):
    return lax.broadcasted_iota(jnp.int32, (n, m), dim)


def _masked_decay(diff, mask):
    return jnp.where(mask, jnp.exp(jnp.where(mask, diff, 0.0)), 0.0)


def _rows_as_lanes(x, n_rows):
    if n_rows < 128:
        x = jnp.concatenate([x, jnp.zeros((128 - n_rows, 128), F32)], axis=0)
    return x.T


def _conv_rows(xp, x_ref, cw_ref, n_rows):
    xp[SUBLANES:SUBLANES + n_rows, :] = x_ref[...]
    base = SUBLANES - (CONV_WIDTH - 1)
    y = xp[base:base + n_rows, :] * cw_ref[0:1, :]
    for i in range(1, CONV_WIDTH):
        y = y + xp[base + i:base + i + n_rows, :] * cw_ref[i:i + 1, :]
    return y


def _each(fn, *lists):
    return [fn(*args) for args in zip(*lists)]


def _unit_lower_solve(a, rhs, n, n_valid):
    if n == CHUNK:
        r, c = _iota2(n, n, 0), _iota2(n, n, 1)
        eye = (r == c).astype(F32)
        same_block = lax.shift_right_logical(r, 4) == lax.shift_right_logical(c, 4)
        sp = lambda xs: [_split2(x) for x in xs]
        times = lambda xs, ys: _each(_mm_x3, xs, ys)
        plus = lambda xs, ys: _each(lambda x, y: x + y, xs, ys)
        d = _each(lambda x: jnp.where(same_block, x, 0.0), a)
        off = _each(lambda x, y: x - y, a, d)
        d_s = sp(d)
        d2_s = sp(times(d_s, d_s))
        t = _each(lambda x: eye - x, d)
        t = plus(t, times(sp(t), d2_s))
        d4_s = sp(times(d2_s, d2_s))
        t = plus(t, times(sp(t), d4_s))
        d8_s = sp(times(d4_s, d4_s))
        t_s = sp(plus(t, times(sp(t), d8_s)))
        b_s = sp(times(t_s, sp(off)))
        y = times(t_s, sp(rhs))
        b2_s = sp(times(b_s, b_s))
        z = _each(lambda yy, by: yy - by, y, times(b_s, sp(y)))
        return plus(z, times(b2_s, sp(z)))
    sols = []
    for ah, rh in zip(a, rhs):
        rows = [rh[0:1]]
        for i in range(1, n_valid):
            x = rh[i:i + 1]
            for j in range(i):
                x = x - ah[i:i + 1, j:j + 1] * rows[j]
            rows.append(x)
        if n_valid < n:
            rows.append(rh[n_valid:n])
        sols.append(jnp.concatenate(rows, axis=0))
    return sols


def _ffn_kernel(x_ref, npre_ref, wg_ref, wu_ref, wd_ref, npost_ref, o_ref, h_ref, a_ref, f_ref):
    x = x_ref[...]
    h_ref[...] = (_rms(x) * npre_ref[...]).astype(BF16)
    for c in range(0, D_FF, MXU_WIDTH):
        cols = slice(c, c + MXU_WIDTH)
        g = jnp.dot(h_ref[...], wg_ref[:, cols], preferred_element_type=F32)
        u = jnp.dot(h_ref[...], wu_ref[:, cols], preferred_element_type=F32)
        a_ref[:, cols] = (_silu(g) * u).astype(BF16)
    for c in range(0, D_MODEL, MXU_WIDTH):
        cols = slice(c, c + MXU_WIDTH)
        f_ref[:, cols] = jnp.dot(a_ref[...], wd_ref[:, cols], preferred_element_type=F32)
    o_ref[...] = x + 0.5 * (_rms(f_ref[...]) * npost_ref[...])


def _ffn(x, npre, wg, wu, wd, npost):
    n = x.shape[0]
    tm = min(ROW_TILE, n)
    const = lambda i: (0, 0)
    return pl.pallas_call(
        _ffn_kernel,
        grid=(n // tm,),
        in_specs=[pl.BlockSpec((tm, D_MODEL), lambda i: (i, 0)),
                  pl.BlockSpec((1, D_MODEL), const),
                  pl.BlockSpec((D_MODEL, D_FF), const),
                  pl.BlockSpec((D_MODEL, D_FF), const),
                  pl.BlockSpec((D_FF, D_MODEL), const),
                  pl.BlockSpec((1, D_MODEL), const)],
        out_specs=pl.BlockSpec((tm, D_MODEL), lambda i: (i, 0)),
        out_shape=jax.ShapeDtypeStruct((n, D_MODEL), F32),
        scratch_shapes=[pltpu.VMEM((tm, D_MODEL), BF16), pltpu.VMEM((tm, D_FF), BF16),
                        pltpu.VMEM((tm, D_MODEL), F32)],
        compiler_params=_cparams(1),
        name=f"ffn_{n}",
    )(x, npre, wg, wu, wd, npost)


def _inproj_kernel(x_ref, n_ref, w_ref, *o_refs):
    h = (_rms(x_ref[...]) * n_ref[...]).astype(BF16)
    col = 0
    for (_, width), o_ref in zip(PROJ_GROUPS, o_refs):
        step = min(width, 512)
        for s in range(0, width, step):
            o_ref[:, s:s + step] = jnp.dot(h, w_ref[:, col + s:col + s + step],
                                           preferred_element_type=F32)
        col += width


def _inproj(x, norm, w):
    n = x.shape[0]
    tm = min(INPROJ_ROW_TILE, n)
    return pl.pallas_call(
        _inproj_kernel,
        grid=(n // tm,),
        in_specs=[pl.BlockSpec((tm, D_MODEL), lambda i: (i, 0)),
                  pl.BlockSpec((1, D_MODEL), lambda i: (0, 0)),
                  pl.BlockSpec((D_MODEL, D_PROJ), lambda i: (0, 0))],
        out_specs=[pl.BlockSpec((tm, wd), lambda i: (i, 0)) for _, wd in PROJ_GROUPS],
        out_shape=[jax.ShapeDtypeStruct((n, wd), F32) for _, wd in PROJ_GROUPS],
        compiler_params=_cparams(1),
        name=f"inproj_{n}",
    )(x, norm, w)


def _merge_kernel(x_ref, og_ref, ol_ref, os_ref, gl_ref, bm_ref, wg_ref, wl_ref, ws_ref, wo_ref,
                  npost_ref, o_ref):
    merged = None
    for i, (b_ref, w_ref) in enumerate(((og_ref, wg_ref), (ol_ref, wl_ref), (os_ref, ws_ref))):
        gate = jax.nn.sigmoid(gl_ref[:, i * D_MODEL:(i + 1) * D_MODEL] + bm_ref[i:i + 1, :])
        term = gate * _mm(b_ref[...], w_ref[...])
        merged = term if merged is None else merged + term
    out = _mm(merged, wo_ref[...])
    o_ref[...] = x_ref[...] + _rms(out) * npost_ref[...]


def _merge(x, o_gdn, o_gla, o_ssd, gates, b_merge, w_gdn, w_gla, w_ssd, w_out, npost):
    n = x.shape[0]
    tm = min(ROW_TILE, n)
    row = lambda w: pl.BlockSpec((tm, w), lambda i: (i, 0))
    full = lambda a: pl.BlockSpec(a.shape, lambda i: (0, 0))
    return pl.pallas_call(
        _merge_kernel,
        grid=(n // tm,),
        in_specs=[row(D_MODEL), row(GDN_V), row(GLA_V), row(SSD_INNER), row(N_GATES),
                  full(b_merge), full(w_gdn), full(w_gla), full(w_ssd), full(w_out), full(npost)],
        out_specs=row(D_MODEL),
        out_shape=jax.ShapeDtypeStruct((n, D_MODEL), F32),
        compiler_params=_cparams(1),
        name=f"merge_{n}",
    )(x, o_gdn, o_gla, o_ssd, gates, b_merge, w_gdn, w_gla, w_ssd, w_out, npost)


def _gdn_core(acts, conv, s_in, params, o_views, *, n, n_valid, chained):
    _, z_views, sm_views = acts
    _, hp_ref, nrm_ref, sel3_ref = params
    m, nh = len(conv), GDN_HEADS
    valid = _iota2(n, 1, 0) < n_valid
    pad = (lambda t: jnp.where(valid, t, 0.0)) if n_valid < n else (lambda t: t)
    r, c = _iota2(n, n, 0), _iota2(n, n, 1)
    causal, strict = r >= c, r > c
    tri = causal.astype(F32).astype(BF16)
    lane = _iota2(n, SMALL_W, 1)

    qkv = [_silu(y) for y in conv]
    sm = [v[...] for v in sm_views]
    g_all = [pad(-jnp.exp(hp_ref[0:1, :]) * _softplus(x + hp_ref[1:2, :])) for x in sm]
    gcum = [_cumsum_rows(tri, g) for g in g_all]
    gcum_t = [_rows_as_lanes(g, n) for g in gcum]
    spread = [_lane_spread(jnp.where(lane < SM_GDN_B, g, pad(jax.nn.sigmoid(x))), sel3_ref)
              for g, x in zip(gcum, sm)]

    units = [(j, h) for j in range(m) for h in range(nh)]
    hcols = [slice(h * GDN_DV, (h + 1) * GDN_DV) for _, h in units]
    gc = [spread[j][:, cs] for (j, _), cs in zip(units, hcols)]
    beta = [spread[j][:, GDN_V + h * GDN_DV:GDN_V + (h + 1) * GDN_DV] for j, h in units]
    gr = [gcum_t[j][SM_GDN_A + h:SM_GDN_A + h + 1, 0:n] for j, h in units]
    q = [pad(_l2n(qkv[j][:, h * GDN_DK:(h + 1) * GDN_DK]) * (GDN_DK ** -0.5)) for j, h in units]
    k = [pad(_l2n(qkv[j][:, GDN_QK + h * GDN_DK:GDN_QK + (h + 1) * GDN_DK])) for j, h in units]
    v = [pad(qkv[j][:, 2 * GDN_QK + h * GDN_DV:2 * GDN_QK + (h + 1) * GDN_DV]) for j, h in units]
    dec = _each(lambda x, y: _masked_decay(x[:, :n] - y, causal), gc, gr)
    eg = _each(jnp.exp, gc)
    kk = _each(_mm_nt, k, k)
    qk = _each(_mm_nt, q, k)
    a = _each(lambda b, x, d: jnp.where(strict, b[:, :n] * x * d, 0.0), beta, kk, dec)
    rhs = _each(lambda b, e, vv, kx: jnp.concatenate([b * vv, (b * e) * kx], axis=-1), beta, eg, v, k)
    sol = _unit_lower_solve(a, rhs, n, n_valid)
    att = _each(lambda x, d: x * d, qk, dec)
    q_in = _each(lambda x, e: x * e, q, eg)
    g_last = [x[n - 1:n, :] for x in gc]
    k_out = _each(lambda kx, gl, x: kx * jnp.exp(gl - x), k, g_last, gc)
    e_last = _each(jnp.exp, g_last)

    def advance(idx, s):
        u = [sol[i][:, :GDN_DV] - _mm(sol[i][:, GDN_DV:], ss) for i, ss in zip(idx, s)]
        o = [_mm(q_in[i], ss) + _mm(att[i], uu) for i, ss, uu in zip(idx, s, u)]
        s_new = [e_last[i] * ss + _mm_tn(k_out[i], uu) for i, ss, uu in zip(idx, s, u)]
        for i, oo in zip(idx, o):
            j, h = units[i]
            cols = slice(h * GDN_DV, (h + 1) * GDN_DV)
            o_views[j][:, cols] = (_rms(oo) * nrm_ref[...]) * _silu(z_views[j][:, cols])
        return s_new

    if chained:
        s = s_in
        for j in range(m):
            s = advance(list(range(j * nh, (j + 1) * nh)), s)
        return s
    s_new = advance(list(range(m * nh)), [s_in[j][h] for j, h in units])
    return [s_new[j * nh:(j + 1) * nh] for j in range(m)]


def _gla_core(acts, conv, s_in, params, o_views, *, n, n_valid, chained):
    qk_views, v_views, g_views, sm_views = acts
    wup_ref, gb_ref, nrm_ref = params
    m, nh = len(qk_views), GLA_HEADS
    rows = _iota2(n, 1, 0)
    valid = rows < n_valid
    pad = (lambda t: jnp.where(valid, t, 0.0)) if n_valid < n else (lambda t: t)
    r, c = _iota2(n, n, 0), _iota2(n, n, 1)
    causal = r >= c
    tri = causal.astype(F32).astype(BF16)
    segs = list(range(m))

    gk = [pad(_log_sigmoid(_mm(sm_views[j][...], wup_ref[...]) + gb_ref[...])
              * (1.0 / GLA_GATE_NORMALIZER)) for j in segs]
    q = [pad(qk_views[j][:, :GLA_QK] * (GLA_DK ** -0.5)) for j in segs]
    k = [pad(qk_views[j][:, GLA_QK:]) for j in segs]
    v = [pad(v_views[j][...]) for j in segs]
    bc = [_cumsum_rows(tri, x) for x in gk]

    units = [(j, h) for j in segs for h in range(nh)]
    hs = [slice(h * GLA_DK, (h + 1) * GLA_DK) for h in range(nh)]
    vs = [slice(h * GLA_DV, (h + 1) * GLA_DV) for h in range(nh)]
    sub = min(GLA_SUB, n)
    att_rows = [[] for _ in units]
    for blk in range(n // sub):
        lo, hi = blk * sub, (blk + 1) * sub
        seen = rows < hi
        for j in segs:
            ref_row = bc[j][lo - 1:lo, :] if blk > 0 else jnp.zeros((1, GLA_QK), F32)
            qa = q[j][lo:hi] * jnp.exp(bc[j][lo:hi] - ref_row)
            ka = jnp.where(seen, k[j] * jnp.exp(jnp.where(seen, ref_row - bc[j], 0.0)), 0.0)
            for h in range(nh):
                att_rows[j * nh + h].append(_mm_nt(qa[:, hs[h]], ka[:, hs[h]]))
    att = [jnp.where(causal, x[0] if len(x) == 1 else jnp.concatenate(x, axis=0), 0.0)
           for x in att_rows]

    b_last = [x[n - 1:n, :] for x in bc]
    q_in = _each(lambda x, b: x * jnp.exp(b), q, bc)
    k_out = _each(lambda x, bl, b: x * jnp.exp(bl - b), k, b_last, bc)
    e_last = _each(jnp.exp, b_last)
    dpair = [[jnp.broadcast_to(e[:, p * 128:(p + 1) * 128], (128, 128)).T for p in range(nh // 2)]
             for e in e_last]
    dcol = [dpair[j][h // 2][(h % 2) * GLA_DK:(h % 2 + 1) * GLA_DK, :] for j, h in units]
    o_att = [_mm(att[i], v[j][:, vs[h]]) for i, (j, h) in enumerate(units)]
    upd = [_mm_tn(k_out[j][:, hs[h]], v[j][:, vs[h]]) for j, h in units]

    def advance(idx, s):
        o = [_mm(q_in[units[i][0]][:, hs[units[i][1]]], ss) + o_att[i] for i, ss in zip(idx, s)]
        s_new = [dcol[i] * ss + upd[i] for i, ss in zip(idx, s)]
        for i, oo in zip(idx, o):
            j, h = units[i]
            o_views[j][:, vs[h]] = (_rms(oo) * nrm_ref[...]) * _silu(g_views[j][:, vs[h]])
        return s_new

    if chained:
        s = s_in
        for j in segs:
            s = advance(list(range(j * nh, (j + 1) * nh)), s)
        return s
    s_new = advance(list(range(m * nh)), [s_in[j][h] for j, h in units])
    return [s_new[j * nh:(j + 1) * nh] for j in segs]


def _ssd_core(acts, conv, s_in, params, o_views, *, n, n_valid, chained):
    z_views, _, sm_views = acts
    _, cb_ref, hp_ref, nrm_ref, sel_p3_ref, sel_n3_ref, d_ref = params
    m, nh = len(conv), SSD_HEADS
    heads_per_group = nh // SSD_GROUPS
    group_w = SSD_INNER // SSD_GROUPS
    valid = _iota2(n, 1, 0) < n_valid
    pad = (lambda t: jnp.where(valid, t, 0.0)) if n_valid < n else (lambda t: t)
    r, c = _iota2(n, n, 0), _iota2(n, n, 1)
    causal = r >= c
    tri = causal.astype(F32).astype(BF16)
    segs = list(range(m))

    xbc = [_silu(y + cb_ref[...]) for y in conv]
    dt_all = [pad(_softplus(sm_views[j][...] + hp_ref[1:2, :])) for j in segs]
    acum = [_cumsum_rows(tri, -jnp.exp(hp_ref[0:1, :]) * x) for x in dt_all]
    acum_t = [_rows_as_lanes(x, n) for x in acum]
    dt_p = [_lane_spread(x, sel_p3_ref) for x in dt_all]
    ac_p = [_lane_spread(x, sel_p3_ref) for x in acum]
    ac_n = [_lane_spread(x, sel_n3_ref) for x in acum]
    xs = [x[:, :SSD_INNER] for x in xbc]
    xdt_all = _each(lambda x, d: x * d, xs, dt_p)
    x_out_all = _each(lambda x, a: x * jnp.exp(a[n - 1:n, :] - a), xdt_all, ac_p)
    skip_all = [d_ref[...] * x for x in xs]
    e_n = _each(jnp.exp, ac_n)
    pairs = [(j, g) for j in segs for g in range(SSD_GROUPS)]
    bm = [pad(xbc[j][:, SSD_INNER + g * SSD_N:SSD_INNER + (g + 1) * SSD_N]) for j, g in pairs]
    cm = [xbc[j][:, SSD_INNER + SSD_BC + g * SSD_N:SSD_INNER + SSD_BC + (g + 1) * SSD_N]
          for j, g in pairs]
    cb = _each(_mm_nt, cm, bm)

    units = [(j, h) for j in segs for h in range(nh)]
    grp = [j * SSD_GROUPS + h // heads_per_group for j, h in units]
    pcols = [slice(h * SSD_P, (h + 1) * SSD_P) for _, h in units]
    ncols = [slice(h * SSD_N, (h + 1) * SSD_N) for _, h in units]
    ar = [acum_t[j][SM_SSD_DT + h:SM_SSD_DT + h + 1, 0:n] for j, h in units]
    dec = [_masked_decay(ac_p[j][:, cs][:, :n] - rr, causal)
           for (j, _), cs, rr in zip(units, pcols, ar)]
    y_att = [_mm(cb[g] * d, xdt_all[j][:, cs]) for g, d, (j, _), cs in zip(grp, dec, units, pcols)]
    c_in = [cm[g] * e_n[j][:, cs] for g, (j, _), cs in zip(grp, units, ncols)]
    upd = [_mm_tn(x_out_all[j][:, cs], bm[g]) for g, (j, _), cs in zip(grp, units, pcols)]
    e_last = [e_n[j][n - 1:n, cs] for (j, _), cs in zip(units, ncols)]
    skip = [skip_all[j][:, cs] for (j, _), cs in zip(units, pcols)]

    def advance(idx, s):
        y = [y_att[i] + _mm_nt(c_in[i], ss) + skip[i] for i, ss in zip(idx, s)]
        s_new = [e_last[i] * ss + upd[i] for i, ss in zip(idx, s)]
        y = [yy * _silu(z_views[units[i][0]][:, units[i][1] * SSD_P:(units[i][1] + 1) * SSD_P])
             for i, yy in zip(idx, y)]
        for p in sorted({grp[i] for i in idx}):
            mine = [(i, yy) for i, yy in zip(idx, y) if grp[i] == p]
            ms = sum(jnp.sum(yy * yy, axis=-1, keepdims=True) for _, yy in mine) * (1.0 / group_w)
            scale = lax.rsqrt(ms + NORM_EPS)
            for i, yy in mine:
                j, h = units[i]
                cols = slice(h * SSD_P, (h + 1) * SSD_P)
                o_views[j][:, cols] = (yy * scale) * nrm_ref[:, cols]
        return s_new

    if chained:
        s = s_in
        for j in segs:
            s = advance(list(range(j * nh, (j + 1) * nh)), s)
        return s
    s_new = advance(list(range(m * nh)), [s_in[j][h] for j, h in units])
    return [s_new[j * nh:(j + 1) * nh] for j in segs]


def _scan_kernel(core, n_act, conv_idx, n_param, n_heads, n, m, nb, aliased):
    has_conv = conv_idx is not None
    chained = nb is None

    def kern(*refs):
        it = iter(refs)
        acts = [next(it) for _ in range(n_act)]
        hist = next(it) if has_conv else None
        s0 = next(it)
        if aliased:
            next(it)
        params = [next(it) for _ in range(n_param)]
        o_ref, s_ref = next(it), next(it)
        xp_ref = next(it) if has_conv else None

        if chained:
            @pl.when(pl.program_id(1) == 0)
            def _():
                s_ref[...] = s0[...]
                if has_conv:
                    xp_ref[0:SUBLANES, :] = hist[...]

            rows = lambda ref: [ref.at[pl.ds(j * n, n)] for j in range(m)]
            conv = None
            if has_conv:
                y = _conv_rows(xp_ref, acts[conv_idx], params[0], m * n)
                xp_ref[0:SUBLANES, :] = xp_ref[m * n:m * n + SUBLANES, :]
                conv = [y[j * n:(j + 1) * n] for j in range(m)]
            s_out = core([rows(a) for a in acts], conv, [s_ref[h] for h in range(n_heads)], params,
                         rows(o_ref), chained=True)
            for h in range(n_heads):
                s_ref[h] = s_out[h]
            return

        def step(i, carry):
            pick = lambda ref: [ref.at[i * m + j] for j in range(m)]
            conv = None
            if has_conv:
                conv = []
                for j, (hv, xv) in enumerate(zip(pick(hist), pick(acts[conv_idx]))):
                    xp = xp_ref.at[j]
                    xp[0:SUBLANES, :] = hv[...]
                    conv.append(_conv_rows(xp, xv, params[0], n))
            s_in = [[sv[h] for h in range(n_heads)] for sv in pick(s0)]
            s_out = core([pick(a) for a in acts], conv, s_in, params, pick(o_ref), chained=False)
            for sv, so in zip(pick(s_ref), s_out):
                for h in range(n_heads):
                    sv[h] = so[h]
            return carry

        lax.fori_loop(0, nb // m, step, 0)

    return kern


def _scan_call(name, core, acts, conv_idx, hist, s0, params, out_w, n, n_valid, layer, s_acc):
    bsz, t, _ = acts[0].shape
    nc = t // n
    state_shape = s0.shape[-3:]
    aliased = s_acc is not None
    if nc > 1:
        m = PROMPT_CHUNKS_PER_STEP if nc % PROMPT_CHUNKS_PER_STEP == 0 else 1
        nb, lead, grid = None, None, (bsz, nc // m)
    else:
        nb = DECODE_SEQS_PER_STEP if bsz % DECODE_SEQS_PER_STEP == 0 else 1
        m = DECODE_SEQS_TOGETHER if nb % DECODE_SEQS_TOGETHER == 0 else 1
        lead, grid = nb, (bsz // nb, 1)
    rows = n * m if nb is None else n

    act_spec = lambda w: pl.BlockSpec((lead, rows, w), lambda b, c: (b, c, 0))
    seq_spec = lambda shape: pl.BlockSpec((lead,) + tuple(shape), lambda b, c: (b,) + (0,) * len(shape))
    if layer is None:
        state_spec = seq_spec(state_shape)
    else:
        state_spec = pl.BlockSpec((None, lead) + tuple(state_shape),
                                  lambda b, c: (layer, b) + (0,) * len(state_shape))
    param_spec = lambda a: pl.BlockSpec(a.shape, lambda b, c: (0,) * a.ndim)

    has_conv = conv_idx is not None
    operands = list(acts) + ([hist] if has_conv else []) + [s0] + ([s_acc] if aliased else [])
    in_specs = ([act_spec(a.shape[-1]) for a in acts]
                + ([seq_spec(hist.shape[1:])] if has_conv else []) + [state_spec]
                + ([pl.BlockSpec(memory_space=pl.ANY)] if aliased else []))
    alias = {len(operands) - 1: 1} if aliased else {}
    operands += list(params)
    in_specs += [param_spec(p) for p in params]
    conv_w = acts[conv_idx].shape[-1] if has_conv else None
    xp_shape = (SUBLANES + rows, conv_w) if nb is None else (m, SUBLANES + rows, conv_w)
    s_out_shape = s_acc.shape if aliased else s0.shape
    kern = _scan_kernel(functools.partial(core, n=n, n_valid=n_valid), len(acts), conv_idx,
                        len(params), state_shape[0], n, m, nb, aliased)
    return pl.pallas_call(
        kern,
        grid=grid,
        in_specs=in_specs,
        out_specs=[act_spec(out_w), state_spec],
        out_shape=[jax.ShapeDtypeStruct((bsz, t, out_w), F32), jax.ShapeDtypeStruct(s_out_shape, F32)],
        scratch_shapes=[pltpu.VMEM(xp_shape, F32)] if has_conv else [],
        input_output_aliases=alias,
        compiler_params=_cparams(2),
        name=f"{name}_{n}",
    )(*operands)


def _lane_row(pairs):
    row = jnp.zeros((SMALL_W,), F32)
    for off, vec in pairs:
        row = row.at[off:off + vec.shape[0]].set(vec.astype(F32))
    return row[None, :]


def _spread_matrix(blocks):
    total = sum(w for _, w in blocks)
    sel = np.zeros((SMALL_W, total), np.float32)
    col = 0
    for src, w in blocks:
        sel[src, col:col + w] = 1.0
        col += w
    return jnp.asarray(np.tile(sel, (3, 1)), BF16)


def _prep_layer(p):
    sizes = (GDN_CONV, GDN_V, GDN_HEADS, GDN_HEADS, GLA_QK, GLA_QK, GLA_V, GLA_V, GLA_RANK,
             SSD_INNER, SSD_CONV, SSD_HEADS, N_GATES)
    offs = [0]
    for s in sizes:
        offs.append(offs[-1] + s)
    (gdn_qkv, gdn_z, gdn_a, gdn_b, gla_q, gla_k, gla_v, gla_g, gla_r, ssd_z, ssd_xbc, ssd_dt,
     gates) = (p["w_in"][:, offs[i]:offs[i + 1]] for i in range(len(sizes)))
    small = jnp.concatenate(
        [gdn_a, gdn_b, gla_r, ssd_dt,
         jnp.zeros((D_MODEL, SMALL_W - 2 * GDN_HEADS - GLA_RANK - SSD_HEADS), F32)], axis=1)
    w_in = jnp.concatenate([gdn_qkv, gdn_z, gla_q, gla_k, gla_v, gla_g, ssd_z, ssd_xbc, gates, small],
                           axis=1).astype(BF16)
    wup = jnp.zeros((SMALL_W, GLA_QK), F32).at[SM_GLA_R:SM_GLA_R + GLA_RANK].set(p["gla_gate_up"])
    row = lambda v: v.astype(F32)[None, :]
    pad_rows = lambda rws: jnp.concatenate(rws + [jnp.zeros((SUBLANES - len(rws), SMALL_W), F32)], 0)
    ffn = lambda i: (row(p[f"ffn{i}_norm_pre"]), p[f"ffn{i}_w_gate"].astype(BF16),
                     p[f"ffn{i}_w_up"].astype(BF16), p[f"ffn{i}_w_down"].astype(BF16),
                     row(p[f"ffn{i}_norm_post"]))
    return dict(
        ffn1=ffn(1), ffn2=ffn(2),
        mix_norm_pre=row(p["mix_norm_pre"]), mix_norm_post=row(p["mix_norm_post"]), w_in=w_in,
        b_merge=p["b_merge"].astype(F32),
        gdn_cw=p["gdn_conv_w"].T.astype(F32),
        gdn_hp=pad_rows([_lane_row([(SM_GDN_A, p["gdn_A_log"])]),
                         _lane_row([(SM_GDN_A, p["gdn_dt_bias"])])]),
        gdn_norm=row(p["gdn_norm"]),
        gdn_sel=_spread_matrix([(SM_GDN_A + h, GDN_DV) for h in range(GDN_HEADS)]
                               + [(SM_GDN_B + h, GDN_DV) for h in range(GDN_HEADS)]),
        gla_wup=wup.astype(BF16), gla_gb=row(p["gla_gate_bias"]), gla_norm=row(p["gla_norm"]),
        ssd_cw=p["ssd_conv_w"].T.astype(F32), ssd_cb=row(p["ssd_conv_b"]),
        ssd_hp=pad_rows([_lane_row([(SM_SSD_DT, p["ssd_A_log"])]),
                         _lane_row([(SM_SSD_DT, p["ssd_dt_bias"])])]),
        ssd_norm=row(p["ssd_norm"]),
        ssd_sel_p=_spread_matrix([(SM_SSD_DT + h, SSD_P) for h in range(SSD_HEADS)]),
        ssd_sel_n=_spread_matrix([(SM_SSD_DT + h, SSD_N) for h in range(SSD_HEADS)]),
        ssd_d=row(jnp.repeat(p["ssd_D"], SSD_P)),
        w_br_gdn=p["w_br_gdn"].astype(BF16), w_br_gla=p["w_br_gla"].astype(BF16),
        w_br_ssd=p["w_br_ssd"].astype(BF16), w_out=p["w_out"].astype(BF16))


def _layer(x, state, w, bsz, t, n, n_valid, layer=None, s_acc=(None, None, None)):
    gdn_hist, s_gdn, s_gla, ssd_hist, s_ssd = state
    x = _ffn(x, *w["ffn1"])
    proj = dict(zip((name for name, _ in PROJ_GROUPS), _inproj(x, w["mix_norm_pre"], w["w_in"])))
    seq = lambda a: a.reshape(bsz, t, a.shape[-1])
    sm = seq(proj["small"])
    o_gdn, s_gdn_new = _scan_call(
        "gdn", _gdn_core, [seq(proj["gdn_qkv"]), seq(proj["gdn_z"]), sm], 0, gdn_hist, s_gdn,
        [w["gdn_cw"], w["gdn_hp"], w["gdn_norm"], w["gdn_sel"]], GDN_V, n, n_valid, layer, s_acc[0])
    o_gla, s_gla_new = _scan_call(
        "gla", _gla_core, [seq(proj["gla_qk"]), seq(proj["gla_v"]), seq(proj["gla_g"]), sm], None,
        None, s_gla, [w["gla_wup"], w["gla_gb"], w["gla_norm"]], GLA_V, n, n_valid, layer, s_acc[1])
    o_ssd, s_ssd_new = _scan_call(
        "ssd", _ssd_core, [seq(proj["ssd_z"]), seq(proj["ssd_xbc"]), sm], 1, ssd_hist, s_ssd,
        [w["ssd_cw"], w["ssd_cb"], w["ssd_hp"], w["ssd_norm"], w["ssd_sel_p"], w["ssd_sel_n"],
         w["ssd_d"]], SSD_INNER, n, n_valid, layer, s_acc[2])
    flat = lambda a: a.reshape(bsz * t, a.shape[-1])
    x = _merge(x, flat(o_gdn), flat(o_gla), flat(o_ssd), proj["gates"], w["b_merge"],
               w["w_br_gdn"], w["w_br_gla"], w["w_br_ssd"], w["w_out"], w["mix_norm_post"])
    x = _ffn(x, *w["ffn2"])
    t_valid = t - (n - n_valid)
    tail = lambda a: seq(a)[:, t_valid - (CONV_WIDTH - 1):t_valid]
    return x, (tail(proj["gdn_qkv"]), s_gdn_new, s_gla_new, tail(proj["ssd_xbc"]), s_ssd_new)


def kernel(x_prompt, x_sample, state_gdn_conv, state_gdn, state_gla, state_ssd_conv, state_ssd,
           ffn1_norm_pre, ffn1_norm_post, ffn1_w_gate, ffn1_w_up, ffn1_w_down,
           mix_norm_pre, mix_norm_post, w_in, b_merge,
           gdn_conv_w, gdn_A_log, gdn_dt_bias, gdn_norm,
           gla_gate_up, gla_gate_bias, gla_norm,
           ssd_conv_w, ssd_conv_b, ssd_A_log, ssd_dt_bias, ssd_D, ssd_norm,
           w_br_gdn, w_br_gla, w_br_ssd, w_out,
           ffn2_norm_pre, ffn2_norm_post, ffn2_w_gate, ffn2_w_up, ffn2_w_down):
    params = dict(
        ffn1_norm_pre=ffn1_norm_pre, ffn1_norm_post=ffn1_norm_post, ffn1_w_gate=ffn1_w_gate,
        ffn1_w_up=ffn1_w_up, ffn1_w_down=ffn1_w_down,
        mix_norm_pre=mix_norm_pre, mix_norm_post=mix_norm_post, w_in=w_in, b_merge=b_merge,
        gdn_conv_w=gdn_conv_w, gdn_A_log=gdn_A_log, gdn_dt_bias=gdn_dt_bias, gdn_norm=gdn_norm,
        gla_gate_up=gla_gate_up, gla_gate_bias=gla_gate_bias, gla_norm=gla_norm,
        ssd_conv_w=ssd_conv_w, ssd_conv_b=ssd_conv_b, ssd_A_log=ssd_A_log, ssd_dt_bias=ssd_dt_bias,
        ssd_D=ssd_D, ssd_norm=ssd_norm,
        w_br_gdn=w_br_gdn, w_br_gla=w_br_gla, w_br_ssd=w_br_ssd, w_out=w_out,
        ffn2_norm_pre=ffn2_norm_pre, ffn2_norm_post=ffn2_norm_post, ffn2_w_gate=ffn2_w_gate,
        ffn2_w_up=ffn2_w_up, ffn2_w_down=ffn2_w_down)
    bp, tp, _ = x_prompt.shape
    bs, ts, _ = x_sample.shape
    ts_pad = SUBLANES
    n_p = min(CHUNK, tp)
    depth = state_gdn.shape[0]

    prompt_init = (jnp.zeros((bp, SUBLANES, GDN_CONV), F32),
                   jnp.zeros((bp, GDN_HEADS, GDN_DK, GDN_DV), F32),
                   jnp.zeros((bp, GLA_HEADS, GLA_DK, GLA_DV), F32),
                   jnp.zeros((bp, SUBLANES, SSD_CONV), F32),
                   jnp.zeros((bp, SSD_HEADS, SSD_P, SSD_N), F32))
    hist_rows = lambda s: jnp.pad(s.astype(F32), ((0, 0), (SUBLANES - (CONV_WIDTH - 1), 0), (0, 0)))
    sample_states = tuple(s.astype(F32) for s in (state_gdn, state_gla, state_ssd))
    s_acc = tuple(jnp.zeros(s.shape, F32) for s in sample_states)

    y_p = x_prompt.reshape(bp * tp, D_MODEL)
    y_s = jnp.pad(x_sample, ((0, 0), (0, ts_pad - ts), (0, 0))).reshape(bs * ts_pad, D_MODEL)
    new_p, s_conv = [], []
    for l in range(depth):
        w = _prep_layer({name: arr[l] for name, arr in params.items()})
        y_p, st_p = _layer(y_p, prompt_init, w, bp, tp, n_p, n_p)
        sample_state = (hist_rows(state_gdn_conv[l]), sample_states[0], sample_states[1],
                        hist_rows(state_ssd_conv[l]), sample_states[2])
        y_s, st_s = _layer(y_s, sample_state, w, bs, ts_pad, ts_pad, ts, layer=l, s_acc=s_acc)
        new_p.append(st_p)
        s_conv.append((st_s[0], st_s[3]))
        s_acc = (st_s[1], st_s[2], st_s[4])
    p_out = tuple(jnp.stack(a) for a in zip(*new_p))
    s_gdn_conv, s_ssd_conv = (jnp.stack(a) for a in zip(*s_conv))
    y_prompt = y_p.reshape(bp, tp, D_MODEL)
    y_sample = y_s.reshape(bs, ts_pad, D_MODEL)[:, :ts]
    return (y_prompt, y_sample) + p_out + (s_gdn_conv, s_acc[0], s_acc[1], s_ssd_conv, s_acc[2])
```

```python
import functools

import jax
import jax.numpy as jnp
import numpy as np
from jax import lax
from jax.experimental import pallas as pl
from jax.experimental.pallas import tpu as pltpu

F32 = jnp.float32
BF16 = jnp.bfloat16

D_MODEL = 1024
DEPTH = 2
D_FF = 2816
MXU_WIDTH = 256
NORM_EPS = 1e-6
CONV_WIDTH = 4
CHUNK = 64
SUBLANES = 8

GDN_HEADS, GDN_DK, GDN_DV = 4, 128, 128
GLA_HEADS, GLA_DK, GLA_DV = 4, 64, 128
GLA_RANK = 16
GLA_GATE_NORMALIZER = 16.0
GLA_SUB = 16
SSD_HEADS, SSD_P, SSD_N, SSD_GROUPS = 8, 64, 128, 2

GDN_QK = GDN_HEADS * GDN_DK
GDN_V = GDN_HEADS * GDN_DV
GDN_CONV = 2 * GDN_QK + GDN_V
GLA_QK = GLA_HEADS * GLA_DK
GLA_V = GLA_HEADS * GLA_DV
SSD_INNER = SSD_HEADS * SSD_P
SSD_BC = SSD_GROUPS * SSD_N
SSD_CONV = SSD_INNER + 2 * SSD_BC
N_GATES = 3 * D_MODEL
SMALL_W = 128
SM_GDN_A, SM_GDN_B, SM_GLA_R, SM_SSD_DT = 0, 4, 8, 24

PROJ_GROUPS = (("gdn_qkv", GDN_CONV), ("ssd_xbc", SSD_CONV), ("gdn_z", GDN_V), ("gla_qk", 2 * GLA_QK),
               ("gla_v", GLA_V), ("gla_g", GLA_V), ("ssd_z", SSD_INNER), ("small", SMALL_W))
D_PROJ = sum(w for _, w in PROJ_GROUPS)
CONV_GROUPS = ("gdn_qkv", "ssd_xbc")

VMEM_LIMIT = 56 * 1024 * 1024
ROW_TILE = 512
INPROJ_ROW_TILE = 256
PROMPT_CHUNKS_PER_STEP = dict(gdn=4, gla=8, ssd=4)
DECODE_SEQS_PER_STEP = 8
DECODE_SEQS_TOGETHER = 4


def _cparams(n_axes):
    return pltpu.CompilerParams(dimension_semantics=("arbitrary",) * n_axes,
                                vmem_limit_bytes=VMEM_LIMIT)


def _mm(a, b):
    return jnp.dot(a.astype(BF16), b.astype(BF16), preferred_element_type=F32)


def _mm_nt(a, b):
    return lax.dot_general(a.astype(BF16), b.astype(BF16), (((1,), (1,)), ((), ())),
                           preferred_element_type=F32)


def _mm_tn(a, b):
    return lax.dot_general(a, b, (((0,), (0,)), ((), ())), preferred_element_type=F32)


def _split2(x):
    hi = x.astype(BF16).astype(F32)
    return hi, (x - hi).astype(BF16).astype(F32)


def _split3_lanes(x):
    hi = x.astype(BF16)
    r = x - hi.astype(F32)
    mid = r.astype(BF16)
    lo = (r - mid.astype(F32)).astype(BF16)
    return jnp.concatenate([hi, mid, lo], axis=1)


def _mm_x3(a2, b2):
    (ah, al), (bh, bl) = a2, b2
    lhs = jnp.concatenate([ah, al, ah], axis=1).astype(BF16)
    rhs = jnp.concatenate([bh, bh, bl], axis=0).astype(BF16)
    return jnp.dot(lhs, rhs, preferred_element_type=F32)


def _cumsum_rows(tri, x):
    w = x.shape[1]
    y = jnp.dot(tri, _split3_lanes(x), preferred_element_type=F32)
    return y[:, :w] + y[:, w:2 * w] + y[:, 2 * w:]


def _lane_spread(x, sel3_ref):
    return jnp.dot(_split3_lanes(x), sel3_ref[...], preferred_element_type=F32)


def _rms(x):
    return x * lax.rsqrt(jnp.mean(x * x, axis=-1, keepdims=True) + NORM_EPS)


def _l2n(x):
    return x * lax.rsqrt(jnp.sum(x * x, axis=-1, keepdims=True) + NORM_EPS)


def _silu(x):
    return x * jax.nn.sigmoid(x)


def _softplus(x):
    return jnp.maximum(x, 0.0) + jnp.log1p(jnp.exp(-jnp.abs(x)))


def _log_sigmoid(x):
    return jnp.minimum(x, 0.0) - jnp.log1p(jnp.exp(-jnp.abs(x)))


def _iota2(n, m, dim):
    return lax.broadcasted_iota(jnp.int32, (n, m), dim)


def _masked_decay(diff, mask):
    return jnp.where(mask, jnp.exp(jnp.where(mask, diff, 0.0)), 0.0)


def _rows_as_lanes(x, n_rows):
    if n_rows < 128:
        x = jnp.concatenate([x, jnp.zeros((128 - n_rows, 128), F32)], axis=0)
    return x.T


def _conv_taps(xp, cw_ref, n_rows):
    base = SUBLANES - (CONV_WIDTH - 1)
    y = xp[base:base + n_rows, :] * cw_ref[0:1, :]
    for i in range(1, CONV_WIDTH):
        y = y + xp[base + i:base + i + n_rows, :] * cw_ref[i:i + 1, :]
    return y


def _conv_rows(xp, x_ref, cw_ref, n_rows):
    xp[SUBLANES:SUBLANES + n_rows, :] = x_ref[...]
    return _conv_taps(xp, cw_ref, n_rows)


def _each(fn, *lists):
    return [fn(*args) for args in zip(*lists)]


def _unit_lower_solve(a, rhs, n, n_valid):
    if n == CHUNK:
        r, c = _iota2(n, n, 0), _iota2(n, n, 1)
        eye = (r == c).astype(F32)
        same_block = lax.shift_right_logical(r, 4) == lax.shift_right_logical(c, 4)
        sp = lambda xs: [_split2(x) for x in xs]
        times = lambda xs, ys: _each(_mm_x3, xs, ys)
        plus = lambda xs, ys: _each(lambda x, y: x + y, xs, ys)
        d = _each(lambda x: jnp.where(same_block, x, 0.0), a)
        off = _each(lambda x, y: x - y, a, d)
        d_s = sp(d)
        d2_s = sp(times(d_s, d_s))
        t = _each(lambda x: eye - x, d)
        t = plus(t, times(sp(t), d2_s))
        d4_s = sp(times(d2_s, d2_s))
        t = plus(t, times(sp(t), d4_s))
        d8_s = sp(times(d4_s, d4_s))
        t_s = sp(plus(t, times(sp(t), d8_s)))
        b_s = sp(times(t_s, sp(off)))
        y = times(t_s, sp(rhs))
        b2_s = sp(times(b_s, b_s))
        z = _each(lambda yy, by: yy - by, y, times(b_s, sp(y)))
        return plus(z, times(b2_s, sp(z)))
    sols = []
    for ah, rh in zip(a, rhs):
        rows = [rh[0:1]]
        for i in range(1, n_valid):
            x = rh[i:i + 1]
            for j in range(i):
                x = x - ah[i:i + 1, j:j + 1] * rows[j]
            rows.append(x)
        if n_valid < n:
            rows.append(rh[n_valid:n])
        sols.append(jnp.concatenate(rows, axis=0))
    return sols


def _ffn_kernel(x_ref, npre_ref, wg_ref, wu_ref, wd_ref, npost_ref, o_ref, h_ref, a_ref, f_ref):
    x = x_ref[...]
    h_ref[...] = (_rms(x) * npre_ref[...]).astype(BF16)
    for c in range(0, D_FF, MXU_WIDTH):
        cols = slice(c, c + MXU_WIDTH)
        g = jnp.dot(h_ref[...], wg_ref[:, cols], preferred_element_type=F32)
        u = jnp.dot(h_ref[...], wu_ref[:, cols], preferred_element_type=F32)
        a_ref[:, cols] = (_silu(g) * u).astype(BF16)
    for c in range(0, D_MODEL, MXU_WIDTH):
        cols = slice(c, c + MXU_WIDTH)
        f_ref[:, cols] = jnp.dot(a_ref[...], wd_ref[:, cols], preferred_element_type=F32)
    o_ref[...] = x + 0.5 * (_rms(f_ref[...]) * npost_ref[...])


def _ffn(x, npre, wg, wu, wd, npost):
    n = x.shape[0]
    tm = min(ROW_TILE, n)
    const = lambda i: (0, 0)
    return pl.pallas_call(
        _ffn_kernel,
        grid=(n // tm,),
        in_specs=[pl.BlockSpec((tm, D_MODEL), lambda i: (i, 0)),
                  pl.BlockSpec((1, D_MODEL), const),
                  pl.BlockSpec((D_MODEL, D_FF), const),
                  pl.BlockSpec((D_MODEL, D_FF), const),
                  pl.BlockSpec((D_FF, D_MODEL), const),
                  pl.BlockSpec((1, D_MODEL), const)],
        out_specs=pl.BlockSpec((tm, D_MODEL), lambda i: (i, 0)),
        out_shape=jax.ShapeDtypeStruct((n, D_MODEL), F32),
        scratch_shapes=[pltpu.VMEM((tm, D_MODEL), BF16), pltpu.VMEM((tm, D_FF), BF16),
                        pltpu.VMEM((tm, D_MODEL), F32)],
        compiler_params=_cparams(1),
        name=f"ffn_{n}",
    )(x, npre, wg, wu, wd, npost)


def _inproj_kernel(*refs, conv, tiles_per_seq, tm):
    x_ref, n_ref, w_ref = refs[:3]
    refs = refs[3:]
    if conv:
        hist = dict(zip(CONV_GROUPS, refs[:2]))
        cw = dict(zip(CONV_GROUPS, refs[2:4]))
        cb_ref = refs[4]
        refs = refs[5:]
    o_refs = dict(zip((name for name, _ in PROJ_GROUPS), refs[:len(PROJ_GROUPS)]))
    refs = refs[len(PROJ_GROUPS):]
    if conv:
        tail = dict(zip(CONV_GROUPS, refs[:2]))
        pre = dict(zip(CONV_GROUPS, refs[2:4]))

        @pl.when(lax.rem(pl.program_id(0), tiles_per_seq) == 0)
        def _():
            for name in CONV_GROUPS:
                pre[name][0:SUBLANES, :] = hist[name][...]

    h = (_rms(x_ref[...]) * n_ref[...]).astype(BF16)
    col = 0
    for name, width in PROJ_GROUPS:
        step = min(width, 512)
        for s in range(0, width, step):
            y = jnp.dot(h, w_ref[:, col + s:col + s + step], preferred_element_type=F32)
            if conv and name in CONV_GROUPS:
                pre[name][SUBLANES:SUBLANES + tm, s:s + step] = y
            else:
                o_refs[name][:, s:s + step] = y
        col += width
    if not conv:
        return
    for name in CONV_GROUPS:
        xp = pre[name]
        y = _conv_taps(xp, cw[name], tm)
        o_refs[name][...] = _silu(y + cb_ref[...] if name == "ssd_xbc" else y)
        tail[name][...] = xp[tm:tm + SUBLANES, :]
        xp[0:SUBLANES, :] = xp[tm:tm + SUBLANES, :]


def _inproj(x, norm, w, conv=None):
    n = x.shape[0]
    tm = min(INPROJ_ROW_TILE, n)
    const = lambda a: pl.BlockSpec(a.shape, lambda i: (0,) * a.ndim)
    operands = [x, norm, w]
    in_specs = [pl.BlockSpec((tm, D_MODEL), lambda i: (i, 0)), const(norm), const(w)]
    out_specs = [pl.BlockSpec((tm, wd), lambda i: (i, 0)) for _, wd in PROJ_GROUPS]
    out_shape = [jax.ShapeDtypeStruct((n, wd), F32) for _, wd in PROJ_GROUPS]
    scratch, tiles_per_seq = [], 1
    if conv is not None:
        t, hist_g, hist_s, cw_g, cw_s, cb_s = conv
        tiles_per_seq = t // tm
        per_seq = lambda a: pl.BlockSpec((None,) + a.shape[1:], lambda i: (i // tiles_per_seq, 0, 0))
        operands += [hist_g, hist_s, cw_g, cw_s, cb_s]
        in_specs += [per_seq(hist_g), per_seq(hist_s), const(cw_g), const(cw_s), const(cb_s)]
        out_specs += [per_seq(hist_g), per_seq(hist_s)]
        out_shape += [jax.ShapeDtypeStruct(hist_g.shape, F32), jax.ShapeDtypeStruct(hist_s.shape, F32)]
        scratch = [pltpu.VMEM((SUBLANES + tm, GDN_CONV), F32), pltpu.VMEM((SUBLANES + tm, SSD_CONV), F32)]
    return pl.pallas_call(
        functools.partial(_inproj_kernel, conv=conv is not None, tiles_per_seq=tiles_per_seq, tm=tm),
        grid=(n // tm,),
        in_specs=in_specs,
        out_specs=out_specs,
        out_shape=out_shape,
        scratch_shapes=scratch,
        compiler_params=_cparams(1),
        name=f"inproj_{n}",
    )(*operands)


def _merge_kernel(x_ref, og_ref, ol_ref, os_ref, npre_ref, wgate_ref, bm_ref, wg_ref, wl_ref, ws_ref,
                  wo_ref, npost_ref, o_ref):
    x = x_ref[...]
    h = (_rms(x) * npre_ref[...]).astype(BF16)
    merged = None
    for i, (b_ref, w_ref) in enumerate(((og_ref, wg_ref), (ol_ref, wl_ref), (os_ref, ws_ref))):
        logits = jnp.dot(h, wgate_ref[:, i * D_MODEL:(i + 1) * D_MODEL], preferred_element_type=F32)
        term = jax.nn.sigmoid(logits + bm_ref[i:i + 1, :]) * _mm(b_ref[...], w_ref[...])
        merged = term if merged is None else merged + term
    out = _mm(merged, wo_ref[...])
    o_ref[...] = x + _rms(out) * npost_ref[...]


def _merge(x, o_gdn, o_gla, o_ssd, npre, w_gate, b_merge, w_gdn, w_gla, w_ssd, w_out, npost):
    n = x.shape[0]
    tm = min(ROW_TILE, n)
    row = lambda w: pl.BlockSpec((tm, w), lambda i: (i, 0))
    full = lambda a: pl.BlockSpec(a.shape, lambda i: (0, 0))
    consts = (npre, w_gate, b_merge, w_gdn, w_gla, w_ssd, w_out, npost)
    return pl.pallas_call(
        _merge_kernel,
        grid=(n // tm,),
        in_specs=[row(D_MODEL), row(GDN_V), row(GLA_V), row(SSD_INNER)] + [full(a) for a in consts],
        out_specs=row(D_MODEL),
        out_shape=jax.ShapeDtypeStruct((n, D_MODEL), F32),
        compiler_params=_cparams(1),
        name=f"merge_{n}",
    )(x, o_gdn, o_gla, o_ssd, *consts)


def _gdn_core(acts, conv, s_in, params, o_views, *, n, n_valid, chained, activated):
    _, z_views, sm_views = acts
    _, hp_ref, nrm_ref, sel3_ref = params
    m, nh = len(conv), GDN_HEADS
    valid = _iota2(n, 1, 0) < n_valid
    pad = (lambda t: jnp.where(valid, t, 0.0)) if n_valid < n else (lambda t: t)
    r, c = _iota2(n, n, 0), _iota2(n, n, 1)
    causal, strict = r >= c, r > c
    tri = causal.astype(F32).astype(BF16)
    lane = _iota2(n, SMALL_W, 1)

    qkv = conv if activated else [_silu(y) for y in conv]
    sm = [v[...] for v in sm_views]
    g_all = [pad(-jnp.exp(hp_ref[0:1, :]) * _softplus(x + hp_ref[1:2, :])) for x in sm]
    gcum = [_cumsum_rows(tri, g) for g in g_all]
    gcum_t = [_rows_as_lanes(g, n) for g in gcum]
    spread = [_lane_spread(jnp.where(lane < SM_GDN_B, g, pad(jax.nn.sigmoid(x))), sel3_ref)
              for g, x in zip(gcum, sm)]

    units = [(j, h) for j in range(m) for h in range(nh)]
    hcols = [slice(h * GDN_DV, (h + 1) * GDN_DV) for _, h in units]
    gc = [spread[j][:, cs] for (j, _), cs in zip(units, hcols)]
    beta = [spread[j][:, GDN_V + h * GDN_DV:GDN_V + (h + 1) * GDN_DV] for j, h in units]
    gr = [gcum_t[j][SM_GDN_A + h:SM_GDN_A + h + 1, 0:n] for j, h in units]
    q = [pad(_l2n(qkv[j][:, h * GDN_DK:(h + 1) * GDN_DK]) * (GDN_DK ** -0.5)) for j, h in units]
    k = [pad(_l2n(qkv[j][:, GDN_QK + h * GDN_DK:GDN_QK + (h + 1) * GDN_DK])) for j, h in units]
    v = [pad(qkv[j][:, 2 * GDN_QK + h * GDN_DV:2 * GDN_QK + (h + 1) * GDN_DV]) for j, h in units]
    dec = _each(lambda x, y: _masked_decay(x[:, :n] - y, causal), gc, gr)
    eg = _each(jnp.exp, gc)
    kk = _each(_mm_nt, k, k)
    qk = _each(_mm_nt, q, k)
    a = _each(lambda b, x, d: jnp.where(strict, b[:, :n] * x * d, 0.0), beta, kk, dec)
    rhs = _each(lambda b, e, vv, kx: jnp.concatenate([b * vv, (b * e) * kx], axis=-1), beta, eg, v, k)
    sol = _unit_lower_solve(a, rhs, n, n_valid)
    att = _each(lambda x, d: x * d, qk, dec)
    q_in = _each(lambda x, e: x * e, q, eg)
    g_last = [x[n - 1:n, :] for x in gc]
    k_out = _each(lambda kx, gl, x: kx * jnp.exp(gl - x), k, g_last, gc)
    e_last = _each(jnp.exp, g_last)

    def advance(idx, s):
        u = [sol[i][:, :GDN_DV] - _mm(sol[i][:, GDN_DV:], ss) for i, ss in zip(idx, s)]
        o = [_mm(q_in[i], ss) + _mm(att[i], uu) for i, ss, uu in zip(idx, s, u)]
        s_new = [e_last[i] * ss + _mm_tn(k_out[i], uu) for i, ss, uu in zip(idx, s, u)]
        for i, oo in zip(idx, o):
            j, h = units[i]
            cols = slice(h * GDN_DV, (h + 1) * GDN_DV)
            o_views[j][:, cols] = (_rms(oo) * nrm_ref[...]) * _silu(z_views[j][:, cols])
        return s_new

    if chained:
        s = s_in
        for j in range(m):
            s = advance(list(range(j * nh, (j + 1) * nh)), s)
        return s
    s_new = advance(list(range(m * nh)), [s_in[j][h] for j, h in units])
    return [s_new[j * nh:(j + 1) * nh] for j in range(m)]


def _gla_core(acts, conv, s_in, params, o_views, *, n, n_valid, chained, activated):
    qk_views, v_views, g_views, sm_views = acts
    wup_ref, gb_ref, nrm_ref = params
    m, nh = len(qk_views), GLA_HEADS
    rows = _iota2(n, 1, 0)
    valid = rows < n_valid
    pad = (lambda t: jnp.where(valid, t, 0.0)) if n_valid < n else (lambda t: t)
    r, c = _iota2(n, n, 0), _iota2(n, n, 1)
    causal = r >= c
    tri = causal.astype(F32).astype(BF16)
    segs = list(range(m))

    gk = [pad(_log_sigmoid(_mm(sm_views[j][...], wup_ref[...]) + gb_ref[...])
              * (1.0 / GLA_GATE_NORMALIZER)) for j in segs]
    q = [pad(qk_views[j][:, :GLA_QK] * (GLA_DK ** -0.5)) for j in segs]
    k = [pad(qk_views[j][:, GLA_QK:]) for j in segs]
    v = [pad(v_views[j][...]) for j in segs]
    bc = [_cumsum_rows(tri, x) for x in gk]

    units = [(j, h) for j in segs for h in range(nh)]
    hs = [slice(h * GLA_DK, (h + 1) * GLA_DK) for h in range(nh)]
    vs = [slice(h * GLA_DV, (h + 1) * GLA_DV) for h in range(nh)]
    sub = min(GLA_SUB, n)
    att_rows = [[] for _ in units]
    for blk in range(n // sub):
        lo, hi = blk * sub, (blk + 1) * sub
        seen = rows < hi
        for j in segs:
            ref_row = bc[j][lo - 1:lo, :] if blk > 0 else jnp.zeros((1, GLA_QK), F32)
            qa = q[j][lo:hi] * jnp.exp(bc[j][lo:hi] - ref_row)
            ka = jnp.where(seen, k[j] * jnp.exp(jnp.where(seen, ref_row - bc[j], 0.0)), 0.0)
            for h in range(nh):
                att_rows[j * nh + h].append(_mm_nt(qa[:, hs[h]], ka[:, hs[h]]))
    att = [jnp.where(causal, x[0] if len(x) == 1 else jnp.concatenate(x, axis=0), 0.0)
           for x in att_rows]

    b_last = [x[n - 1:n, :] for x in bc]
    q_in = _each(lambda x, b: x * jnp.exp(b), q, bc)
    k_out = _each(lambda x, bl, b: x * jnp.exp(bl - b), k, b_last, bc)
    e_last = _each(jnp.exp, b_last)
    dpair = [[jnp.broadcast_to(e[:, p * 128:(p + 1) * 128], (128, 128)).T for p in range(nh // 2)]
             for e in e_last]
    dcol = [dpair[j][h // 2][(h % 2) * GLA_DK:(h % 2 + 1) * GLA_DK, :] for j, h in units]
    o_att = [_mm(att[i], v[j][:, vs[h]]) for i, (j, h) in enumerate(units)]
    upd = [_mm_tn(k_out[j][:, hs[h]], v[j][:, vs[h]]) for j, h in units]

    def advance(idx, s):
        o = [_mm(q_in[units[i][0]][:, hs[units[i][1]]], ss) + o_att[i] for i, ss in zip(idx, s)]
        s_new = [dcol[i] * ss + upd[i] for i, ss in zip(idx, s)]
        for i, oo in zip(idx, o):
            j, h = units[i]
            o_views[j][:, vs[h]] = (_rms(oo) * nrm_ref[...]) * _silu(g_views[j][:, vs[h]])
        return s_new

    if chained:
        s = s_in
        for j in segs:
            s = advance(list(range(j * nh, (j + 1) * nh)), s)
        return s
    s_new = advance(list(range(m * nh)), [s_in[j][h] for j, h in units])
    return [s_new[j * nh:(j + 1) * nh] for j in segs]


def _ssd_core(acts, conv, s_in, params, o_views, *, n, n_valid, chained, activated):
    z_views, _, sm_views = acts
    _, cb_ref, hp_ref, nrm_ref, sel_p3_ref, sel_n3_ref, d_ref = params
    m, nh = len(conv), SSD_HEADS
    heads_per_group = nh // SSD_GROUPS
    group_w = SSD_INNER // SSD_GROUPS
    valid = _iota2(n, 1, 0) < n_valid
    pad = (lambda t: jnp.where(valid, t, 0.0)) if n_valid < n else (lambda t: t)
    r, c = _iota2(n, n, 0), _iota2(n, n, 1)
    causal = r >= c
    tri = causal.astype(F32).astype(BF16)
    segs = list(range(m))

    xbc = conv if activated else [_silu(y + cb_ref[...]) for y in conv]
    dt_all = [pad(_softplus(sm_views[j][...] + hp_ref[1:2, :])) for j in segs]
    acum = [_cumsum_rows(tri, -jnp.exp(hp_ref[0:1, :]) * x) for x in dt_all]
    acum_t = [_rows_as_lanes(x, n) for x in acum]
    dt_p = [_lane_spread(x, sel_p3_ref) for x in dt_all]
    ac_p = [_lane_spread(x, sel_p3_ref) for x in acum]
    ac_n = [_lane_spread(x, sel_n3_ref) for x in acum]
    xs = [x[:, :SSD_INNER] for x in xbc]
    xdt_all = _each(lambda x, d: x * d, xs, dt_p)
    x_out_all = _each(lambda x, a: x * jnp.exp(a[n - 1:n, :] - a), xdt_all, ac_p)
    skip_all = [d_ref[...] * x for x in xs]
    e_n = _each(jnp.exp, ac_n)
    pairs = [(j, g) for j in segs for g in range(SSD_GROUPS)]
    bm = [pad(xbc[j][:, SSD_INNER + g * SSD_N:SSD_INNER + (g + 1) * SSD_N]) for j, g in pairs]
    cm = [xbc[j][:, SSD_INNER + SSD_BC + g * SSD_N:SSD_INNER + SSD_BC + (g + 1) * SSD_N]
          for j, g in pairs]
    cb = _each(_mm_nt, cm, bm)

    units = [(j, h) for j in segs for h in range(nh)]
    grp = [j * SSD_GROUPS + h // heads_per_group for j, h in units]
    pcols = [slice(h * SSD_P, (h + 1) * SSD_P) for _, h in units]
    ncols = [slice(h * SSD_N, (h + 1) * SSD_N) for _, h in units]
    ar = [acum_t[j][SM_SSD_DT + h:SM_SSD_DT + h + 1, 0:n] for j, h in units]
    dec = [_masked_decay(ac_p[j][:, cs][:, :n] - rr, causal)
           for (j, _), cs, rr in zip(units, pcols, ar)]
    y_att = [_mm(cb[g] * d, xdt_all[j][:, cs]) for g, d, (j, _), cs in zip(grp, dec, units, pcols)]
    c_in = [cm[g] * e_n[j][:, cs] for g, (j, _), cs in zip(grp, units, ncols)]
    upd = [_mm_tn(x_out_all[j][:, cs], bm[g]) for g, (j, _), cs in zip(grp, units, pcols)]
    e_last = [e_n[j][n - 1:n, cs] for (j, _), cs in zip(units, ncols)]
    skip = [skip_all[j][:, cs] for (j, _), cs in zip(units, pcols)]

    def advance(idx, s):
        y = [y_att[i] + _mm_nt(c_in[i], ss) + skip[i] for i, ss in zip(idx, s)]
        s_new = [e_last[i] * ss + upd[i] for i, ss in zip(idx, s)]
        y = [yy * _silu(z_views[units[i][0]][:, units[i][1] * SSD_P:(units[i][1] + 1) * SSD_P])
             for i, yy in zip(idx, y)]
        for p in sorted({grp[i] for i in idx}):
            mine = [(i, yy) for i, yy in zip(idx, y) if grp[i] == p]
            ms = sum(jnp.sum(yy * yy, axis=-1, keepdims=True) for _, yy in mine) * (1.0 / group_w)
            scale = lax.rsqrt(ms + NORM_EPS)
            for i, yy in mine:
                j, h = units[i]
                cols = slice(h * SSD_P, (h + 1) * SSD_P)
                o_views[j][:, cols] = (yy * scale) * nrm_ref[:, cols]
        return s_new

    if chained:
        s = s_in
        for j in segs:
            s = advance(list(range(j * nh, (j + 1) * nh)), s)
        return s
    s_new = advance(list(range(m * nh)), [s_in[j][h] for j, h in units])
    return [s_new[j * nh:(j + 1) * nh] for j in segs]


def _scan_kernel(core, n_act, conv_idx, has_conv, n_param, n_heads, n, m, nb, aliased):
    chained = nb is None
    activated = conv_idx is not None and not has_conv
    core = functools.partial(core, activated=activated)

    def kern(*refs):
        it = iter(refs)
        acts = [next(it) for _ in range(n_act)]
        hist = next(it) if has_conv else None
        s0 = next(it)
        if aliased:
            next(it)
        params = [next(it) for _ in range(n_param)]
        o_ref, s_ref = next(it), next(it)
        xp_ref = next(it) if has_conv else None

        if chained:
            @pl.when(pl.program_id(1) == 0)
            def _():
                s_ref[...] = s0[...]
                if has_conv:
                    xp_ref[0:SUBLANES, :] = hist[...]

            rows = lambda ref: [ref.at[pl.ds(j * n, n)] for j in range(m)]
            conv = None
            if has_conv:
                y = _conv_rows(xp_ref, acts[conv_idx], params[0], m * n)
                xp_ref[0:SUBLANES, :] = xp_ref[m * n:m * n + SUBLANES, :]
                conv = [y[j * n:(j + 1) * n] for j in range(m)]
            elif activated:
                conv = [v[...] for v in rows(acts[conv_idx])]
            s_out = core([rows(a) for a in acts], conv, [s_ref[h] for h in range(n_heads)], params,
                         rows(o_ref), chained=True)
            for h in range(n_heads):
                s_ref[h] = s_out[h]
            return

        def step(i, carry):
            pick = lambda ref: [ref.at[i * m + j] for j in range(m)]
            conv = None
            if has_conv:
                conv = []
                for j, (hv, xv) in enumerate(zip(pick(hist), pick(acts[conv_idx]))):
                    xp = xp_ref.at[j]
                    xp[0:SUBLANES, :] = hv[...]
                    conv.append(_conv_rows(xp, xv, params[0], n))
            elif activated:
                conv = [v[...] for v in pick(acts[conv_idx])]
            s_in = [[sv[h] for h in range(n_heads)] for sv in pick(s0)]
            s_out = core([pick(a) for a in acts], conv, s_in, params, pick(o_ref), chained=False)
            for sv, so in zip(pick(s_ref), s_out):
                for h in range(n_heads):
                    sv[h] = so[h]
            return carry

        lax.fori_loop(0, nb // m, step, 0)

    return kern


def _scan_call(name, core, acts, conv_idx, hist, s0, params, out_w, n, n_valid, layer, s_acc):
    bsz, t, _ = acts[0].shape
    nc = t // n
    state_shape = s0.shape[-3:]
    aliased = s_acc is not None
    if nc > 1:
        m = PROMPT_CHUNKS_PER_STEP[name] if nc % PROMPT_CHUNKS_PER_STEP[name] == 0 else 1
        nb, lead, grid = None, None, (bsz, nc // m)
    else:
        nb = DECODE_SEQS_PER_STEP if bsz % DECODE_SEQS_PER_STEP == 0 else 1
        m = DECODE_SEQS_TOGETHER if nb % DECODE_SEQS_TOGETHER == 0 else 1
        lead, grid = nb, (bsz // nb, 1)
    rows = n * m if nb is None else n

    act_spec = lambda w: pl.BlockSpec((lead, rows, w), lambda b, c: (b, c, 0))
    seq_spec = lambda shape: pl.BlockSpec((lead,) + tuple(shape), lambda b, c: (b,) + (0,) * len(shape))
    if layer is None:
        state_spec = seq_spec(state_shape)
    else:
        state_spec = pl.BlockSpec((None, lead) + tuple(state_shape),
                                  lambda b, c: (layer, b) + (0,) * len(state_shape))
    param_spec = lambda a: pl.BlockSpec(a.shape, lambda b, c: (0,) * a.ndim)

    has_conv = hist is not None
    operands = list(acts) + ([hist] if has_conv else []) + [s0] + ([s_acc] if aliased else [])
    in_specs = ([act_spec(a.shape[-1]) for a in acts]
                + ([seq_spec(hist.shape[1:])] if has_conv else []) + [state_spec]
                + ([pl.BlockSpec(memory_space=pl.ANY)] if aliased else []))
    alias = {len(operands) - 1: 1} if aliased else {}
    operands += list(params)
    in_specs += [param_spec(p) for p in params]
    conv_w = acts[conv_idx].shape[-1] if has_conv else None
    xp_shape = (SUBLANES + rows, conv_w) if nb is None else (m, SUBLANES + rows, conv_w)
    s_out_shape = s_acc.shape if aliased else s0.shape
    kern = _scan_kernel(functools.partial(core, n=n, n_valid=n_valid), len(acts), conv_idx, has_conv,
                        len(params), state_shape[0], n, m, nb, aliased)
    return pl.pallas_call(
        kern,
        grid=grid,
        in_specs=in_specs,
        out_specs=[act_spec(out_w), state_spec],
        out_shape=[jax.ShapeDtypeStruct((bsz, t, out_w), F32), jax.ShapeDtypeStruct(s_out_shape, F32)],
        scratch_shapes=[pltpu.VMEM(xp_shape, F32)] if has_conv else [],
        input_output_aliases=alias,
        compiler_params=_cparams(2),
        name=f"{name}_{n}",
    )(*operands)


def _lane_row(pairs):
    row = jnp.zeros((SMALL_W,), F32)
    for off, vec in pairs:
        row = row.at[off:off + vec.shape[0]].set(vec.astype(F32))
    return row[None, :]


def _spread_matrix(blocks):
    total = sum(w for _, w in blocks)
    sel = np.zeros((SMALL_W, total), np.float32)
    col = 0
    for src, w in blocks:
        sel[src, col:col + w] = 1.0
        col += w
    return jnp.asarray(np.tile(sel, (3, 1)), BF16)


def _prep_layer(p):
    sizes = (GDN_CONV, GDN_V, GDN_HEADS, GDN_HEADS, GLA_QK, GLA_QK, GLA_V, GLA_V, GLA_RANK,
             SSD_INNER, SSD_CONV, SSD_HEADS, N_GATES)
    offs = [0]
    for s in sizes:
        offs.append(offs[-1] + s)
    (gdn_qkv, gdn_z, gdn_a, gdn_b, gla_q, gla_k, gla_v, gla_g, gla_r, ssd_z, ssd_xbc, ssd_dt,
     gates) = (p["w_in"][:, offs[i]:offs[i + 1]] for i in range(len(sizes)))
    small = jnp.concatenate(
        [gdn_a, gdn_b, gla_r, ssd_dt,
         jnp.zeros((D_MODEL, SMALL_W - 2 * GDN_HEADS - GLA_RANK - SSD_HEADS), F32)], axis=1)
    cols = dict(gdn_qkv=gdn_qkv, gdn_z=gdn_z, gla_qk=jnp.concatenate([gla_q, gla_k], axis=1),
                gla_v=gla_v, gla_g=gla_g, ssd_z=ssd_z, ssd_xbc=ssd_xbc, small=small)
    w_in = jnp.concatenate([cols[name] for name, _ in PROJ_GROUPS], axis=1).astype(BF16)
    wup = jnp.zeros((SMALL_W, GLA_QK), F32).at[SM_GLA_R:SM_GLA_R + GLA_RANK].set(p["gla_gate_up"])
    row = lambda v: v.astype(F32)[None, :]
    pad_rows = lambda rws: jnp.concatenate(rws + [jnp.zeros((SUBLANES - len(rws), SMALL_W), F32)], 0)
    ffn = lambda i: (row(p[f"ffn{i}_norm_pre"]), p[f"ffn{i}_w_gate"].astype(BF16),
                     p[f"ffn{i}_w_up"].astype(BF16), p[f"ffn{i}_w_down"].astype(BF16),
                     row(p[f"ffn{i}_norm_post"]))
    return dict(
        ffn1=ffn(1), ffn2=ffn(2),
        mix_norm_pre=row(p["mix_norm_pre"]), mix_norm_post=row(p["mix_norm_post"]), w_in=w_in,
        w_gate=gates.astype(BF16), b_merge=p["b_merge"].astype(F32),
        gdn_cw=p["gdn_conv_w"].T.astype(F32),
        gdn_hp=pad_rows([_lane_row([(SM_GDN_A, p["gdn_A_log"])]),
                         _lane_row([(SM_GDN_A, p["gdn_dt_bias"])])]),
        gdn_norm=row(p["gdn_norm"]),
        gdn_sel=_spread_matrix([(SM_GDN_A + h, GDN_DV) for h in range(GDN_HEADS)]
                               + [(SM_GDN_B + h, GDN_DV) for h in range(GDN_HEADS)]),
        gla_wup=wup.astype(BF16), gla_gb=row(p["gla_gate_bias"]), gla_norm=row(p["gla_norm"]),
        ssd_cw=p["ssd_conv_w"].T.astype(F32), ssd_cb=row(p["ssd_conv_b"]),
        ssd_hp=pad_rows([_lane_row([(SM_SSD_DT, p["ssd_A_log"])]),
                         _lane_row([(SM_SSD_DT, p["ssd_dt_bias"])])]),
        ssd_norm=row(p["ssd_norm"]),
        ssd_sel_p=_spread_matrix([(SM_SSD_DT + h, SSD_P) for h in range(SSD_HEADS)]),
        ssd_sel_n=_spread_matrix([(SM_SSD_DT + h, SSD_N) for h in range(SSD_HEADS)]),
        ssd_d=row(jnp.repeat(p["ssd_D"], SSD_P)),
        w_br_gdn=p["w_br_gdn"].astype(BF16), w_br_gla=p["w_br_gla"].astype(BF16),
        w_br_ssd=p["w_br_ssd"].astype(BF16), w_out=p["w_out"].astype(BF16))


def _layer(x, state, w, bsz, t, n, n_valid, layer=None, s_acc=(None, None, None)):
    gdn_hist, s_gdn, s_gla, ssd_hist, s_ssd = state
    x = _ffn(x, *w["ffn1"])
    fuse_conv = t > n and t % min(INPROJ_ROW_TILE, bsz * t) == 0
    conv = (t, gdn_hist, ssd_hist, w["gdn_cw"], w["ssd_cw"], w["ssd_cb"]) if fuse_conv else None
    outs = _inproj(x, w["mix_norm_pre"], w["w_in"], conv)
    proj = dict(zip((name for name, _ in PROJ_GROUPS), outs))
    seq = lambda a: a.reshape(bsz, t, a.shape[-1])
    sm = seq(proj["small"])
    t_valid = t - (n - n_valid)
    if fuse_conv:
        gdn_hist = ssd_hist = None
        tails = [a[:, SUBLANES - (CONV_WIDTH - 1):] for a in outs[len(PROJ_GROUPS):]]
    else:
        tails = [seq(proj[g])[:, t_valid - (CONV_WIDTH - 1):t_valid] for g in CONV_GROUPS]
    o_gdn, s_gdn_new = _scan_call(
        "gdn", _gdn_core, [seq(proj["gdn_qkv"]), seq(proj["gdn_z"]), sm], 0, gdn_hist, s_gdn,
        [w["gdn_cw"], w["gdn_hp"], w["gdn_norm"], w["gdn_sel"]], GDN_V, n, n_valid, layer, s_acc[0])
    o_gla, s_gla_new = _scan_call(
        "gla", _gla_core, [seq(proj["gla_qk"]), seq(proj["gla_v"]), seq(proj["gla_g"]), sm], None,
        None, s_gla, [w["gla_wup"], w["gla_gb"], w["gla_norm"]], GLA_V, n, n_valid, layer, s_acc[1])
    o_ssd, s_ssd_new = _scan_call(
        "ssd", _ssd_core, [seq(proj["ssd_z"]), seq(proj["ssd_xbc"]), sm], 1, ssd_hist, s_ssd,
        [w["ssd_cw"], w["ssd_cb"], w["ssd_hp"], w["ssd_norm"], w["ssd_sel_p"], w["ssd_sel_n"],
         w["ssd_d"]], SSD_INNER, n, n_valid, layer, s_acc[2])
    flat = lambda a: a.reshape(bsz * t, a.shape[-1])
    x = _merge(x, flat(o_gdn), flat(o_gla), flat(o_ssd), w["mix_norm_pre"], w["w_gate"], w["b_merge"],
               w["w_br_gdn"], w["w_br_gla"], w["w_br_ssd"], w["w_out"], w["mix_norm_post"])
    x = _ffn(x, *w["ffn2"])
    return x, (tails[0], s_gdn_new, s_gla_new, tails[1], s_ssd_new)


def kernel(x_prompt, x_sample, state_gdn_conv, state_gdn, state_gla, state_ssd_conv, state_ssd,
           ffn1_norm_pre, ffn1_norm_post, ffn1_w_gate, ffn1_w_up, ffn1_w_down,
           mix_norm_pre, mix_norm_post, w_in, b_merge,
           gdn_conv_w, gdn_A_log, gdn_dt_bias, gdn_norm,
           gla_gate_up, gla_gate_bias, gla_norm,
           ssd_conv_w, ssd_conv_b, ssd_A_log, ssd_dt_bias, ssd_D, ssd_norm,
           w_br_gdn, w_br_gla, w_br_ssd, w_out,
           ffn2_norm_pre, ffn2_norm_post, ffn2_w_gate, ffn2_w_up, ffn2_w_down):
    params = dict(
        ffn1_norm_pre=ffn1_norm_pre, ffn1_norm_post=ffn1_norm_post, ffn1_w_gate=ffn1_w_gate,
        ffn1_w_up=ffn1_w_up, ffn1_w_down=ffn1_w_down,
        mix_norm_pre=mix_norm_pre, mix_norm_post=mix_norm_post, w_in=w_in, b_merge=b_merge,
        gdn_conv_w=gdn_conv_w, gdn_A_log=gdn_A_log, gdn_dt_bias=gdn_dt_bias, gdn_norm=gdn_norm,
        gla_gate_up=gla_gate_up, gla_gate_bias=gla_gate_bias, gla_norm=gla_norm,
        ssd_conv_w=ssd_conv_w, ssd_conv_b=ssd_conv_b, ssd_A_log=ssd_A_log, ssd_dt_bias=ssd_dt_bias,
        ssd_D=ssd_D, ssd_norm=ssd_norm,
        w_br_gdn=w_br_gdn, w_br_gla=w_br_gla, w_br_ssd=w_br_ssd, w_out=w_out,
        ffn2_norm_pre=ffn2_norm_pre, ffn2_norm_post=ffn2_norm_post, ffn2_w_gate=ffn2_w_gate,
        ffn2_w_up=ffn2_w_up, ffn2_w_down=ffn2_w_down)
    bp, tp, _ = x_prompt.shape
    bs, ts, _ = x_sample.shape
    ts_pad = SUBLANES
    n_p = min(CHUNK, tp)
    depth = state_gdn.shape[0]

    prompt_init = (jnp.zeros((bp, SUBLANES, GDN_CONV), F32),
                   jnp.zeros((bp, GDN_HEADS, GDN_DK, GDN_DV), F32),
                   jnp.zeros((bp, GLA_HEADS, GLA_DK, GLA_DV), F32),
                   jnp.zeros((bp, SUBLANES, SSD_CONV), F32),
                   jnp.zeros((bp, SSD_HEADS, SSD_P, SSD_N), F32))
    hist_rows = lambda s: jnp.pad(s.astype(F32), ((0, 0), (SUBLANES - (CONV_WIDTH - 1), 0), (0, 0)))
    sample_states = tuple(s.astype(F32) for s in (state_gdn, state_gla, state_ssd))
    s_acc = tuple(jnp.zeros(s.shape, F32) for s in sample_states)

    y_p = x_prompt.reshape(bp * tp, D_MODEL)
    y_s = jnp.pad(x_sample, ((0, 0), (0, ts_pad - ts), (0, 0))).reshape(bs * ts_pad, D_MODEL)
    new_p, s_conv = [], []
    for l in range(depth):
        w = _prep_layer({name: arr[l] for name, arr in params.items()})
        y_p, st_p = _layer(y_p, prompt_init, w, bp, tp, n_p, n_p)
        sample_state = (hist_rows(state_gdn_conv[l]), sample_states[0], sample_states[1],
                        hist_rows(state_ssd_conv[l]), sample_states[2])
        y_s, st_s = _layer(y_s, sample_state, w, bs, ts_pad, ts_pad, ts, layer=l, s_acc=s_acc)
        new_p.append(st_p)
        s_conv.append((st_s[0], st_s[3]))
        s_acc = (st_s[1], st_s[2], st_s[4])
    p_out = tuple(jnp.stack(a) for a in zip(*new_p))
    s_gdn_conv, s_ssd_conv = (jnp.stack(a) for a in zip(*s_conv))
    y_prompt = y_p.reshape(bp, tp, D_MODEL)
    y_sample = y_s.reshape(bs, ts_pad, D_MODEL)[:, :ts]
    return (y_prompt, y_sample) + p_out + (s_gdn_conv, s_acc[0], s_acc[1], s_ssd_conv, s_acc[2])
```

```python
import functools

import jax
import jax.numpy as jnp
import numpy as np
from jax import lax
from jax.experimental import pallas as pl
from jax.experimental.pallas import tpu as pltpu

F32 = jnp.float32
BF16 = jnp.bfloat16

D_MODEL = 1024
DEPTH = 2
D_FF = 2816
MXU_WIDTH = 256
NORM_EPS = 1e-6
CONV_WIDTH = 4
CHUNK = 64
SUBLANES = 8

GDN_HEADS, GDN_DK, GDN_DV = 4, 128, 128
GLA_HEADS, GLA_DK, GLA_DV = 4, 64, 128
GLA_RANK = 16
GLA_GATE_NORMALIZER = 16.0
GLA_SUB = 16
SSD_HEADS, SSD_P, SSD_N, SSD_GROUPS = 8, 64, 128, 2

GDN_QK = GDN_HEADS * GDN_DK
GDN_V = GDN_HEADS * GDN_DV
GDN_CONV = 2 * GDN_QK + GDN_V
GLA_QK = GLA_HEADS * GLA_DK
GLA_V = GLA_HEADS * GLA_DV
SSD_INNER = SSD_HEADS * SSD_P
SSD_BC = SSD_GROUPS * SSD_N
SSD_CONV = SSD_INNER + 2 * SSD_BC
N_GATES = 3 * D_MODEL
SMALL_W = 128
SM_GDN_A, SM_GDN_B, SM_GLA_R, SM_SSD_DT = 0, 4, 8, 24

PROJ_GROUPS = (("gdn_qkv", GDN_CONV), ("ssd_xbc", SSD_CONV), ("gdn_z", GDN_V), ("gla_qk", 2 * GLA_QK),
               ("gla_v", GLA_V), ("gla_g", GLA_V), ("ssd_z", SSD_INNER), ("small", SMALL_W))
D_PROJ = sum(w for _, w in PROJ_GROUPS)
CONV_GROUPS = ("gdn_qkv", "ssd_xbc")

VMEM_LIMIT = 56 * 1024 * 1024
ROW_TILE = 512
INPROJ_ROW_TILE = 256
PROMPT_CHUNKS_PER_STEP = dict(gdn=4, gla=8, ssd=4)
DECODE_SEQS_PER_STEP = 8
DECODE_SEQS_TOGETHER = 8


def _cparams(n_axes):
    return pltpu.CompilerParams(dimension_semantics=("arbitrary",) * n_axes,
                                vmem_limit_bytes=VMEM_LIMIT)


def _mm(a, b):
    return jnp.dot(a.astype(BF16), b.astype(BF16), preferred_element_type=F32)


def _mm_nt(a, b):
    return lax.dot_general(a.astype(BF16), b.astype(BF16), (((1,), (1,)), ((), ())),
                           preferred_element_type=F32)


def _mm_tn(a, b):
    return lax.dot_general(a, b, (((0,), (0,)), ((), ())), preferred_element_type=F32)


def _split2(x):
    hi = x.astype(BF16).astype(F32)
    return hi, (x - hi).astype(BF16).astype(F32)


def _split3_lanes(x):
    hi = x.astype(BF16)
    r = x - hi.astype(F32)
    mid = r.astype(BF16)
    lo = (r - mid.astype(F32)).astype(BF16)
    return jnp.concatenate([hi, mid, lo], axis=1)


def _mm_x3(a2, b2):
    (ah, al), (bh, bl) = a2, b2
    lhs = jnp.concatenate([ah, al, ah], axis=1).astype(BF16)
    rhs = jnp.concatenate([bh, bh, bl], axis=0).astype(BF16)
    return jnp.dot(lhs, rhs, preferred_element_type=F32)


def _cumsum_rows(tri, x):
    w = x.shape[1]
    y = jnp.dot(tri, _split3_lanes(x), preferred_element_type=F32)
    return y[:, :w] + y[:, w:2 * w] + y[:, 2 * w:]


def _lane_spread(x, sel3_ref):
    return jnp.dot(_split3_lanes(x), sel3_ref[...], preferred_element_type=F32)


def _rms(x):
    return x * lax.rsqrt(jnp.mean(x * x, axis=-1, keepdims=True) + NORM_EPS)


def _l2n(x):
    return x * lax.rsqrt(jnp.sum(x * x, axis=-1, keepdims=True) + NORM_EPS)


def _silu(x):
    return x * jax.nn.sigmoid(x)


def _softplus(x):
    return jnp.maximum(x, 0.0) + jnp.log1p(jnp.exp(-jnp.abs(x)))


def _log_sigmoid(x):
    return jnp.minimum(x, 0.0) - jnp.log1p(jnp.exp(-jnp.abs(x)))


def _iota2(n, m, dim):
    return lax.broadcasted_iota(jnp.int32, (n, m), dim)


def _masked_decay(diff, mask):
    return jnp.where(mask, jnp.exp(jnp.where(mask, diff, 0.0)), 0.0)


def _rows_as_lanes(x, n_rows):
    if n_rows < 128:
        x = jnp.concatenate([x, jnp.zeros((128 - n_rows, 128), F32)], axis=0)
    return x.T


def _conv_taps(xp, cw_ref, n_rows):
    x = xp[0:SUBLANES + n_rows, :]
    z = x * cw_ref[0:1, :]
    for i in range(1, CONV_WIDTH):
        z = pltpu.roll(z, 1, axis=0) + x * cw_ref[i:i + 1, :]
    return z[SUBLANES:, :]


def _conv_rows(xp, x_ref, cw_ref, n_rows):
    xp[SUBLANES:SUBLANES + n_rows, :] = x_ref[...]
    return _conv_taps(xp, cw_ref, n_rows)


def _each(fn, *lists):
    return [fn(*args) for args in zip(*lists)]


def _unit_lower_solve(a, rhs, n, n_valid):
    if n == CHUNK:
        r, c = _iota2(n, n, 0), _iota2(n, n, 1)
        eye = (r == c).astype(F32)
        same_block = lax.shift_right_logical(r, 4) == lax.shift_right_logical(c, 4)
        sp = lambda xs: [_split2(x) for x in xs]
        times = lambda xs, ys: _each(_mm_x3, xs, ys)
        plus = lambda xs, ys: _each(lambda x, y: x + y, xs, ys)
        d = _each(lambda x: jnp.where(same_block, x, 0.0), a)
        off = _each(lambda x, y: x - y, a, d)
        d_s = sp(d)
        d2_s = sp(times(d_s, d_s))
        t = _each(lambda x: eye - x, d)
        t = plus(t, times(sp(t), d2_s))
        d4_s = sp(times(d2_s, d2_s))
        t = plus(t, times(sp(t), d4_s))
        d8_s = sp(times(d4_s, d4_s))
        t_s = sp(plus(t, times(sp(t), d8_s)))
        b_s = sp(times(t_s, sp(off)))
        y = times(t_s, sp(rhs))
        b2_s = sp(times(b_s, b_s))
        z = _each(lambda yy, by: yy - by, y, times(b_s, sp(y)))
        return plus(z, times(b2_s, sp(z)))
    sols = []
    for ah, rh in zip(a, rhs):
        rows = [rh[0:1]]
        for i in range(1, n_valid):
            x = rh[i:i + 1]
            for j in range(i):
                x = x - ah[i:i + 1, j:j + 1] * rows[j]
            rows.append(x)
        if n_valid < n:
            rows.append(rh[n_valid:n])
        sols.append(jnp.concatenate(rows, axis=0))
    return sols


def _ffn_kernel(x_ref, npre_ref, wg_ref, wu_ref, wd_ref, npost_ref, o_ref, h_ref, a_ref, f_ref):
    x = x_ref[...]
    h_ref[...] = (_rms(x) * npre_ref[...]).astype(BF16)
    for c in range(0, D_FF, MXU_WIDTH):
        cols = slice(c, c + MXU_WIDTH)
        g = jnp.dot(h_ref[...], wg_ref[:, cols], preferred_element_type=F32)
        u = jnp.dot(h_ref[...], wu_ref[:, cols], preferred_element_type=F32)
        a_ref[:, cols] = (_silu(g) * u).astype(BF16)
    for c in range(0, D_MODEL, MXU_WIDTH):
        cols = slice(c, c + MXU_WIDTH)
        f_ref[:, cols] = jnp.dot(a_ref[...], wd_ref[:, cols], preferred_element_type=F32)
    o_ref[...] = x + 0.5 * (_rms(f_ref[...]) * npost_ref[...])


def _of_layer(a, layer):
    return pl.BlockSpec((None,) + a.shape[1:], lambda *_: (layer, 0, 0))


def _ffn(x, npre, wg, wu, wd, npost, layer):
    n = x.shape[0]
    tm = min(ROW_TILE, n)
    const = lambda i: (0, 0)
    return pl.pallas_call(
        _ffn_kernel,
        grid=(n // tm,),
        in_specs=[pl.BlockSpec((tm, D_MODEL), lambda i: (i, 0)),
                  pl.BlockSpec((1, D_MODEL), const),
                  _of_layer(wg, layer), _of_layer(wu, layer), _of_layer(wd, layer),
                  pl.BlockSpec((1, D_MODEL), const)],
        out_specs=pl.BlockSpec((tm, D_MODEL), lambda i: (i, 0)),
        out_shape=jax.ShapeDtypeStruct((n, D_MODEL), F32),
        scratch_shapes=[pltpu.VMEM((tm, D_MODEL), BF16), pltpu.VMEM((tm, D_FF), BF16),
                        pltpu.VMEM((tm, D_MODEL), F32)],
        compiler_params=_cparams(1),
        name=f"ffn_{n}",
    )(x, npre, wg, wu, wd, npost)


def _inproj_kernel(*refs, conv, tiles_per_seq, tm):
    x_ref, n_ref, w_ref = refs[:3]
    refs = refs[3:]
    if conv:
        hist = dict(zip(CONV_GROUPS, refs[:2]))
        cw = dict(zip(CONV_GROUPS, refs[2:4]))
        cb_ref = refs[4]
        refs = refs[5:]
    o_refs = dict(zip((name for name, _ in PROJ_GROUPS), refs[:len(PROJ_GROUPS)]))
    refs = refs[len(PROJ_GROUPS):]
    if conv:
        tail = dict(zip(CONV_GROUPS, refs[:2]))
        pre = dict(zip(CONV_GROUPS, refs[2:4]))

        @pl.when(lax.rem(pl.program_id(0), tiles_per_seq) == 0)
        def _():
            for name in CONV_GROUPS:
                pre[name][0:SUBLANES, :] = hist[name][...]

    h = (_rms(x_ref[...]) * n_ref[...]).astype(BF16)
    col = 0
    for name, width in PROJ_GROUPS:
        step = min(width, 512)
        for s in range(0, width, step):
            y = jnp.dot(h, w_ref[:, col + s:col + s + step], preferred_element_type=F32)
            if conv and name in CONV_GROUPS:
                pre[name][SUBLANES:SUBLANES + tm, s:s + step] = y
            else:
                o_refs[name][:, s:s + step] = y
        col += width
    if not conv:
        return
    for name in CONV_GROUPS:
        xp = pre[name]
        y = _conv_taps(xp, cw[name], tm)
        o_refs[name][...] = _silu(y + cb_ref[...] if name == "ssd_xbc" else y)
        tail[name][...] = xp[tm:tm + SUBLANES, :]
        xp[0:SUBLANES, :] = xp[tm:tm + SUBLANES, :]


def _inproj(x, norm, w, layer, conv=None):
    n = x.shape[0]
    tm = min(INPROJ_ROW_TILE, n)
    const = lambda a: pl.BlockSpec(a.shape, lambda i: (0,) * a.ndim)
    operands = [x, norm, w]
    in_specs = [pl.BlockSpec((tm, D_MODEL), lambda i: (i, 0)), const(norm), _of_layer(w, layer)]
    out_specs = [pl.BlockSpec((tm, wd), lambda i: (i, 0)) for _, wd in PROJ_GROUPS]
    out_shape = [jax.ShapeDtypeStruct((n, wd), F32) for _, wd in PROJ_GROUPS]
    scratch, tiles_per_seq = [], 1
    if conv is not None:
        t, hist_g, hist_s, cw_g, cw_s, cb_s = conv
        tiles_per_seq = t // tm
        per_seq = lambda a: pl.BlockSpec((None,) + a.shape[1:], lambda i: (i // tiles_per_seq, 0, 0))
        operands += [hist_g, hist_s, cw_g, cw_s, cb_s]
        in_specs += [per_seq(hist_g), per_seq(hist_s), const(cw_g), const(cw_s), const(cb_s)]
        out_specs += [per_seq(hist_g), per_seq(hist_s)]
        out_shape += [jax.ShapeDtypeStruct(hist_g.shape, F32), jax.ShapeDtypeStruct(hist_s.shape, F32)]
        scratch = [pltpu.VMEM((SUBLANES + tm, GDN_CONV), F32), pltpu.VMEM((SUBLANES + tm, SSD_CONV), F32)]
    return pl.pallas_call(
        functools.partial(_inproj_kernel, conv=conv is not None, tiles_per_seq=tiles_per_seq, tm=tm),
        grid=(n // tm,),
        in_specs=in_specs,
        out_specs=out_specs,
        out_shape=out_shape,
        scratch_shapes=scratch,
        compiler_params=_cparams(1),
        name=f"inproj_{n}",
    )(*operands)


def _merge_kernel(x_ref, og_ref, ol_ref, os_ref, npre_ref, wgate_ref, bm_ref, wg_ref, wl_ref, ws_ref,
                  wo_ref, npost_ref, o_ref):
    x = x_ref[...]
    h = (_rms(x) * npre_ref[...]).astype(BF16)
    merged = None
    for i, (b_ref, w_ref) in enumerate(((og_ref, wg_ref), (ol_ref, wl_ref), (os_ref, ws_ref))):
        logits = jnp.dot(h, wgate_ref[:, i * D_MODEL:(i + 1) * D_MODEL], preferred_element_type=F32)
        term = jax.nn.sigmoid(logits + bm_ref[i:i + 1, :]) * _mm(b_ref[...], w_ref[...])
        merged = term if merged is None else merged + term
    out = _mm(merged, wo_ref[...])
    o_ref[...] = x + _rms(out) * npost_ref[...]


def _merge(x, o_gdn, o_gla, o_ssd, npre, w_gate, b_merge, w_gdn, w_gla, w_ssd, w_out, npost, layer):
    n = x.shape[0]
    tm = min(ROW_TILE, n)
    row = lambda w: pl.BlockSpec((tm, w), lambda i: (i, 0))
    full = lambda a: _of_layer(a, layer) if a.ndim == 3 else pl.BlockSpec(a.shape, lambda i: (0, 0))
    consts = (npre, w_gate, b_merge, w_gdn, w_gla, w_ssd, w_out, npost)
    return pl.pallas_call(
        _merge_kernel,
        grid=(n // tm,),
        in_specs=[row(D_MODEL), row(GDN_V), row(GLA_V), row(SSD_INNER)] + [full(a) for a in consts],
        out_specs=row(D_MODEL),
        out_shape=jax.ShapeDtypeStruct((n, D_MODEL), F32),
        compiler_params=_cparams(1),
        name=f"merge_{n}",
    )(x, o_gdn, o_gla, o_ssd, *consts)


def _gdn_core(acts, conv, s_in, params, o_views, *, n, n_valid, chained, activated):
    _, z_views, sm_views = acts
    _, hp_ref, nrm_ref, sel3_ref = params
    m, nh = len(conv), GDN_HEADS
    valid = _iota2(n, 1, 0) < n_valid
    pad = (lambda t: jnp.where(valid, t, 0.0)) if n_valid < n else (lambda t: t)
    r, c = _iota2(n, n, 0), _iota2(n, n, 1)
    causal, strict = r >= c, r > c
    lane = _iota2(n, SMALL_W, 1)

    qkv = conv if activated else [_silu(y) for y in conv]
    sm = [v[...] for v in sm_views]
    g_all = [pad(-jnp.exp(hp_ref[0:1, :]) * _softplus(x + hp_ref[1:2, :])) for x in sm]
    tri = causal.astype(F32).astype(BF16)
    gcum = [_cumsum_rows(tri, g) for g in g_all]
    gcum_t = [_rows_as_lanes(g, n) for g in gcum]
    units = [(j, h) for j in range(m) for h in range(nh)]
    beta_all = [pad(jax.nn.sigmoid(x)) for x in sm]
    if n == CHUNK:
        spread = [_lane_spread(jnp.where(lane < SM_GDN_B, g, b), sel3_ref)
                  for g, b in zip(gcum, beta_all)]
        gc = [spread[j][:, h * GDN_DV:(h + 1) * GDN_DV] for j, h in units]
        beta = [spread[j][:, GDN_V + h * GDN_DV:GDN_V + (h + 1) * GDN_DV] for j, h in units]
    else:
        wide = lambda col: jnp.broadcast_to(col, (n, GDN_DV))
        gc = [wide(gcum[j][:, SM_GDN_A + h:SM_GDN_A + h + 1]) for j, h in units]
        beta = [wide(beta_all[j][:, SM_GDN_B + h:SM_GDN_B + h + 1]) for j, h in units]
    gr = [gcum_t[j][SM_GDN_A + h:SM_GDN_A + h + 1, 0:n] for j, h in units]
    q = [pad(_l2n(qkv[j][:, h * GDN_DK:(h + 1) * GDN_DK]) * (GDN_DK ** -0.5)) for j, h in units]
    k = [pad(_l2n(qkv[j][:, GDN_QK + h * GDN_DK:GDN_QK + (h + 1) * GDN_DK])) for j, h in units]
    v = [pad(qkv[j][:, 2 * GDN_QK + h * GDN_DV:2 * GDN_QK + (h + 1) * GDN_DV]) for j, h in units]
    dec = _each(lambda x, y: _masked_decay(x[:, :n] - y, causal), gc, gr)
    eg = _each(jnp.exp, gc)
    kk = _each(_mm_nt, k, k)
    qk = _each(_mm_nt, q, k)
    a = _each(lambda b, x, d: jnp.where(strict, b[:, :n] * x * d, 0.0), beta, kk, dec)
    rhs = _each(lambda b, e, vv, kx: jnp.concatenate([b * vv, (b * e) * kx], axis=-1), beta, eg, v, k)
    sol = _unit_lower_solve(a, rhs, n, n_valid)
    att = _each(lambda x, d: x * d, qk, dec)
    q_in = _each(lambda x, e: x * e, q, eg)
    g_last = [x[n - 1:n, :] for x in gc]
    k_out = _each(lambda kx, gl, x: kx * jnp.exp(gl - x), k, g_last, gc)
    e_last = _each(jnp.exp, g_last)

    def advance(idx, s):
        u = [sol[i][:, :GDN_DV] - _mm(sol[i][:, GDN_DV:], ss) for i, ss in zip(idx, s)]
        o = [_mm(q_in[i], ss) + _mm(att[i], uu) for i, ss, uu in zip(idx, s, u)]
        s_new = [e_last[i] * ss + _mm_tn(k_out[i], uu) for i, ss, uu in zip(idx, s, u)]
        for i, oo in zip(idx, o):
            j, h = units[i]
            cols = slice(h * GDN_DV, (h + 1) * GDN_DV)
            o_views[j][:, cols] = (_rms(oo) * nrm_ref[...]) * _silu(z_views[j][:, cols])
        return s_new

    if chained:
        s = s_in
        for j in range(m):
            s = advance(list(range(j * nh, (j + 1) * nh)), s)
        return s
    s_new = advance(list(range(m * nh)), [s_in[j][h] for j, h in units])
    return [s_new[j * nh:(j + 1) * nh] for j in range(m)]


def _gla_core(acts, conv, s_in, params, o_views, *, n, n_valid, chained, activated):
    qk_views, v_views, g_views, sm_views = acts
    wup_ref, gb_ref, nrm_ref = params
    m, nh = len(qk_views), GLA_HEADS
    rows = _iota2(n, 1, 0)
    valid = rows < n_valid
    pad = (lambda t: jnp.where(valid, t, 0.0)) if n_valid < n else (lambda t: t)
    r, c = _iota2(n, n, 0), _iota2(n, n, 1)
    causal = r >= c
    segs = list(range(m))

    gk = [pad(_log_sigmoid(_mm(sm_views[j][...], wup_ref[...]) + gb_ref[...])
              * (1.0 / GLA_GATE_NORMALIZER)) for j in segs]
    q = [pad(qk_views[j][:, :GLA_QK] * (GLA_DK ** -0.5)) for j in segs]
    k = [pad(qk_views[j][:, GLA_QK:]) for j in segs]
    v = [pad(v_views[j][...]) for j in segs]
    tri = causal.astype(F32).astype(BF16)
    bc = [_cumsum_rows(tri, x) for x in gk]

    units = [(j, h) for j in segs for h in range(nh)]
    hs = [slice(h * GLA_DK, (h + 1) * GLA_DK) for h in range(nh)]
    vs = [slice(h * GLA_DV, (h + 1) * GLA_DV) for h in range(nh)]
    sub = min(GLA_SUB, n)
    att_rows = [[] for _ in units]
    for blk in range(n // sub):
        lo, hi = blk * sub, (blk + 1) * sub
        seen = rows < hi
        for j in segs:
            ref_row = bc[j][lo - 1:lo, :] if blk > 0 else jnp.zeros((1, GLA_QK), F32)
            qa = q[j][lo:hi] * jnp.exp(bc[j][lo:hi] - ref_row)
            ka = jnp.where(seen, k[j] * jnp.exp(jnp.where(seen, ref_row - bc[j], 0.0)), 0.0)
            for h in range(nh):
                att_rows[j * nh + h].append(_mm_nt(qa[:, hs[h]], ka[:, hs[h]]))
    att = [jnp.where(causal, x[0] if len(x) == 1 else jnp.concatenate(x, axis=0), 0.0)
           for x in att_rows]

    b_last = [x[n - 1:n, :] for x in bc]
    q_in = _each(lambda x, b: x * jnp.exp(b), q, bc)
    k_out = _each(lambda x, bl, b: x * jnp.exp(bl - b), k, b_last, bc)
    e_last = _each(jnp.exp, b_last)
    dpair = [[jnp.broadcast_to(e[:, p * 128:(p + 1) * 128], (128, 128)).T for p in range(nh // 2)]
             for e in e_last]
    dcol = [dpair[j][h // 2][(h % 2) * GLA_DK:(h % 2 + 1) * GLA_DK, :] for j, h in units]
    o_att = [_mm(att[i], v[j][:, vs[h]]) for i, (j, h) in enumerate(units)]
    upd = [_mm_tn(k_out[j][:, hs[h]], v[j][:, vs[h]]) for j, h in units]

    def emit(idx, s):
        o = [_mm(q_in[units[i][0]][:, hs[units[i][1]]], ss) + o_att[i] for i, ss in zip(idx, s)]
        for i, oo in zip(idx, o):
            j, h = units[i]
            o_views[j][:, vs[h]] = (_rms(oo) * nrm_ref[...]) * _silu(g_views[j][:, vs[h]])

    step = lambda idx, s: [dcol[i] * ss + upd[i] for i, ss in zip(idx, s)]
    everything = list(range(m * nh))
    if chained:
        starts, s = [], s_in
        for j in segs:
            starts += s
            s = step(list(range(j * nh, (j + 1) * nh)), s)
        emit(everything, starts)
        return s
    starts = [s_in[j][h] for j, h in units]
    emit(everything, starts)
    s_new = step(everything, starts)
    return [s_new[j * nh:(j + 1) * nh] for j in segs]


def _ssd_core(acts, conv, s_in, params, o_views, *, n, n_valid, chained, activated):
    z_views, _, sm_views = acts
    _, cb_ref, hp_ref, nrm_ref, sel_p3_ref, sel_n3_ref, d_ref = params
    m, nh = len(conv), SSD_HEADS
    heads_per_group = nh // SSD_GROUPS
    group_w = SSD_INNER // SSD_GROUPS
    valid = _iota2(n, 1, 0) < n_valid
    pad = (lambda t: jnp.where(valid, t, 0.0)) if n_valid < n else (lambda t: t)
    r, c = _iota2(n, n, 0), _iota2(n, n, 1)
    causal = r >= c
    segs = list(range(m))

    xbc = conv if activated else [_silu(y + cb_ref[...]) for y in conv]
    dt_all = [pad(_softplus(sm_views[j][...] + hp_ref[1:2, :])) for j in segs]
    tri = causal.astype(F32).astype(BF16)
    acum = [_cumsum_rows(tri, -jnp.exp(hp_ref[0:1, :]) * x) for x in dt_all]
    acum_t = [_rows_as_lanes(x, n) for x in acum]
    dt_p = [_lane_spread(x, sel_p3_ref) for x in dt_all]
    ac_p = [_lane_spread(x, sel_p3_ref) for x in acum]
    ac_n = [_lane_spread(x, sel_n3_ref) for x in acum]
    xs = [x[:, :SSD_INNER] for x in xbc]
    xdt_all = _each(lambda x, d: x * d, xs, dt_p)
    x_out_all = _each(lambda x, a: x * jnp.exp(a[n - 1:n, :] - a), xdt_all, ac_p)
    skip_all = [d_ref[...] * x for x in xs]
    e_n = _each(jnp.exp, ac_n)
    pairs = [(j, g) for j in segs for g in range(SSD_GROUPS)]
    bm = [pad(xbc[j][:, SSD_INNER + g * SSD_N:SSD_INNER + (g + 1) * SSD_N]) for j, g in pairs]
    cm = [xbc[j][:, SSD_INNER + SSD_BC + g * SSD_N:SSD_INNER + SSD_BC + (g + 1) * SSD_N]
          for j, g in pairs]
    cb = _each(_mm_nt, cm, bm)

    units = [(j, h) for j in segs for h in range(nh)]
    grp = [j * SSD_GROUPS + h // heads_per_group for j, h in units]
    pcols = [slice(h * SSD_P, (h + 1) * SSD_P) for _, h in units]
    ncols = [slice(h * SSD_N, (h + 1) * SSD_N) for _, h in units]
    ar = [acum_t[j][SM_SSD_DT + h:SM_SSD_DT + h + 1, 0:n] for j, h in units]
    dec = [_masked_decay(ac_p[j][:, cs][:, :n] - rr, causal)
           for (j, _), cs, rr in zip(units, pcols, ar)]
    y_att = [_mm(cb[g] * d, xdt_all[j][:, cs]) for g, d, (j, _), cs in zip(grp, dec, units, pcols)]
    c_in = [cm[g] * e_n[j][:, cs] for g, (j, _), cs in zip(grp, units, ncols)]
    upd = [_mm_tn(x_out_all[j][:, cs], bm[g]) for g, (j, _), cs in zip(grp, units, pcols)]
    e_last = [e_n[j][n - 1:n, cs] for (j, _), cs in zip(units, ncols)]
    skip = [skip_all[j][:, cs] for (j, _), cs in zip(units, pcols)]

    def emit(idx, s):
        y = [y_att[i] + _mm_nt(c_in[i], ss) + skip[i] for i, ss in zip(idx, s)]
        y = [yy * _silu(z_views[units[i][0]][:, units[i][1] * SSD_P:(units[i][1] + 1) * SSD_P])
             for i, yy in zip(idx, y)]
        for p in sorted({grp[i] for i in idx}):
            mine = [(i, yy) for i, yy in zip(idx, y) if grp[i] == p]
            ms = sum(jnp.sum(yy * yy, axis=-1, keepdims=True) for _, yy in mine) * (1.0 / group_w)
            scale = lax.rsqrt(ms + NORM_EPS)
            for i, yy in mine:
                j, h = units[i]
                cols = slice(h * SSD_P, (h + 1) * SSD_P)
                o_views[j][:, cols] = (yy * scale) * nrm_ref[:, cols]

    step = lambda idx, s: [e_last[i] * ss + upd[i] for i, ss in zip(idx, s)]
    everything = list(range(m * nh))
    if chained:
        starts, s = [], s_in
        for j in segs:
            starts += s
            s = step(list(range(j * nh, (j + 1) * nh)), s)
        emit(everything, starts)
        return s
    starts = [s_in[j][h] for j, h in units]
    emit(everything, starts)
    s_new = step(everything, starts)
    return [s_new[j * nh:(j + 1) * nh] for j in segs]


def _scan_kernel(core, n_act, conv_idx, has_conv, n_param, n_heads, n, m, nb, aliased):
    chained = nb is None
    activated = conv_idx is not None and not has_conv
    core = functools.partial(core, activated=activated)

    def kern(*refs):
        it = iter(refs)
        acts = [next(it) for _ in range(n_act)]
        hist = next(it) if has_conv else None
        s0 = next(it)
        if aliased:
            next(it)
        params = [next(it) for _ in range(n_param)]
        o_ref, s_ref = next(it), next(it)
        xp_ref = next(it) if has_conv else None

        if chained:
            @pl.when(pl.program_id(1) == 0)
            def _():
                s_ref[...] = s0[...]
                if has_conv:
                    xp_ref[0:SUBLANES, :] = hist[...]

            rows = lambda ref: [ref.at[pl.ds(j * n, n)] for j in range(m)]
            conv = None
            if has_conv:
                y = _conv_rows(xp_ref, acts[conv_idx], params[0], m * n)
                xp_ref[0:SUBLANES, :] = xp_ref[m * n:m * n + SUBLANES, :]
                conv = [y[j * n:(j + 1) * n] for j in range(m)]
            elif activated:
                conv = [v[...] for v in rows(acts[conv_idx])]
            s_out = core([rows(a) for a in acts], conv, [s_ref[h] for h in range(n_heads)], params,
                         rows(o_ref), chained=True)
            for h in range(n_heads):
                s_ref[h] = s_out[h]
            return

        def step(i, carry):
            pick = lambda ref: [ref.at[i * m + j] for j in range(m)]
            conv = None
            if has_conv:
                conv = []
                for j, (hv, xv) in enumerate(zip(pick(hist), pick(acts[conv_idx]))):
                    xp = xp_ref.at[j]
                    xp[0:SUBLANES, :] = hv[...]
                    conv.append(_conv_rows(xp, xv, params[0], n))
            elif activated:
                conv = [v[...] for v in pick(acts[conv_idx])]
            s_in = [[sv[h] for h in range(n_heads)] for sv in pick(s0)]
            s_out = core([pick(a) for a in acts], conv, s_in, params, pick(o_ref), chained=False)
            for sv, so in zip(pick(s_ref), s_out):
                for h in range(n_heads):
                    sv[h] = so[h]
            return carry

        lax.fori_loop(0, nb // m, step, 0)

    return kern


def _scan_call(name, core, acts, conv_idx, hist, s0, params, out_w, n, n_valid, layer, s_acc):
    bsz, t, _ = acts[0].shape
    nc = t // n
    state_shape = s0.shape[-3:]
    aliased = s_acc is not None
    if nc > 1:
        m = PROMPT_CHUNKS_PER_STEP[name] if nc % PROMPT_CHUNKS_PER_STEP[name] == 0 else 1
        nb, lead, grid = None, None, (bsz, nc // m)
    else:
        nb = DECODE_SEQS_PER_STEP if bsz % DECODE_SEQS_PER_STEP == 0 else 1
        m = DECODE_SEQS_TOGETHER if nb % DECODE_SEQS_TOGETHER == 0 else 1
        lead, grid = nb, (bsz // nb, 1)
    rows = n * m if nb is None else n

    act_spec = lambda w: pl.BlockSpec((lead, rows, w), lambda b, c: (b, c, 0))
    seq_spec = lambda shape: pl.BlockSpec((lead,) + tuple(shape), lambda b, c: (b,) + (0,) * len(shape))
    if layer is None:
        state_spec = seq_spec(state_shape)
    else:
        state_spec = pl.BlockSpec((None, lead) + tuple(state_shape),
                                  lambda b, c: (layer, b) + (0,) * len(state_shape))
    param_spec = lambda a: pl.BlockSpec(a.shape, lambda b, c: (0,) * a.ndim)

    has_conv = hist is not None
    operands = list(acts) + ([hist] if has_conv else []) + [s0] + ([s_acc] if aliased else [])
    in_specs = ([act_spec(a.shape[-1]) for a in acts]
                + ([seq_spec(hist.shape[1:])] if has_conv else []) + [state_spec]
                + ([pl.BlockSpec(memory_space=pl.ANY)] if aliased else []))
    alias = {len(operands) - 1: 1} if aliased else {}
    operands += list(params)
    in_specs += [param_spec(p) for p in params]
    conv_w = acts[conv_idx].shape[-1] if has_conv else None
    xp_shape = (SUBLANES + rows, conv_w) if nb is None else (m, SUBLANES + rows, conv_w)
    s_out_shape = s_acc.shape if aliased else s0.shape
    kern = _scan_kernel(functools.partial(core, n=n, n_valid=n_valid), len(acts), conv_idx, has_conv,
                        len(params), state_shape[0], n, m, nb, aliased)
    return pl.pallas_call(
        kern,
        grid=grid,
        in_specs=in_specs,
        out_specs=[act_spec(out_w), state_spec],
        out_shape=[jax.ShapeDtypeStruct((bsz, t, out_w), F32), jax.ShapeDtypeStruct(s_out_shape, F32)],
        scratch_shapes=[pltpu.VMEM(xp_shape, F32)] if has_conv else [],
        input_output_aliases=alias,
        compiler_params=_cparams(2),
        name=f"{name}_{n}",
    )(*operands)


def _lane_row(pairs):
    row = jnp.zeros((SMALL_W,), F32)
    for off, vec in pairs:
        row = row.at[off:off + vec.shape[0]].set(vec.astype(F32))
    return row[None, :]


def _spread_matrix(blocks):
    total = sum(w for _, w in blocks)
    sel = np.zeros((SMALL_W, total), np.float32)
    col = 0
    for src, w in blocks:
        sel[src, col:col + w] = 1.0
        col += w
    return jnp.asarray(np.tile(sel, (3, 1)), BF16)


def _prep_matrices(params):
    sizes = (GDN_CONV, GDN_V, GDN_HEADS, GDN_HEADS, GLA_QK, GLA_QK, GLA_V, GLA_V, GLA_RANK,
             SSD_INNER, SSD_CONV, SSD_HEADS, N_GATES)
    offs = [0]
    for s in sizes:
        offs.append(offs[-1] + s)
    w_all = params["w_in"]
    (gdn_qkv, gdn_z, gdn_a, gdn_b, gla_q, gla_k, gla_v, gla_g, gla_r, ssd_z, ssd_xbc, ssd_dt,
     gates) = (w_all[..., offs[i]:offs[i + 1]] for i in range(len(sizes)))
    small = jnp.concatenate(
        [gdn_a, gdn_b, gla_r, ssd_dt,
         jnp.zeros(w_all.shape[:2] + (SMALL_W - 2 * GDN_HEADS - GLA_RANK - SSD_HEADS,), F32)], axis=-1)
    cols = dict(gdn_qkv=gdn_qkv, gdn_z=gdn_z, gla_qk=jnp.concatenate([gla_q, gla_k], axis=-1),
                gla_v=gla_v, gla_g=gla_g, ssd_z=ssd_z, ssd_xbc=ssd_xbc, small=small)
    mats = dict(w_in=jnp.concatenate([cols[name] for name, _ in PROJ_GROUPS], axis=-1).astype(BF16),
                w_gate=gates.astype(BF16))
    for name in ("ffn1_w_gate", "ffn1_w_up", "ffn1_w_down", "ffn2_w_gate", "ffn2_w_up", "ffn2_w_down",
                 "w_br_gdn", "w_br_gla", "w_br_ssd", "w_out"):
        mats[name] = params[name].astype(BF16)
    return mats


def _prep_layer(p, mats):
    wup = jnp.zeros((SMALL_W, GLA_QK), F32).at[SM_GLA_R:SM_GLA_R + GLA_RANK].set(p["gla_gate_up"])
    row = lambda v: v.astype(F32)[None, :]
    pad_rows = lambda rws: jnp.concatenate(rws + [jnp.zeros((SUBLANES - len(rws), SMALL_W), F32)], 0)
    ffn = lambda i: (row(p[f"ffn{i}_norm_pre"]), mats[f"ffn{i}_w_gate"], mats[f"ffn{i}_w_up"],
                     mats[f"ffn{i}_w_down"], row(p[f"ffn{i}_norm_post"]))
    return dict(
        ffn1=ffn(1), ffn2=ffn(2),
        mix_norm_pre=row(p["mix_norm_pre"]), mix_norm_post=row(p["mix_norm_post"]), w_in=mats["w_in"],
        w_gate=mats["w_gate"], b_merge=p["b_merge"].astype(F32),
        gdn_cw=p["gdn_conv_w"].T.astype(F32),
        gdn_hp=pad_rows([_lane_row([(SM_GDN_A, p["gdn_A_log"])]),
                         _lane_row([(SM_GDN_A, p["gdn_dt_bias"])])]),
        gdn_norm=row(p["gdn_norm"]),
        gdn_sel=_spread_matrix([(SM_GDN_A + h, GDN_DV) for h in range(GDN_HEADS)]
                               + [(SM_GDN_B + h, GDN_DV) for h in range(GDN_HEADS)]),
        gla_wup=wup.astype(BF16), gla_gb=row(p["gla_gate_bias"]), gla_norm=row(p["gla_norm"]),
        ssd_cw=p["ssd_conv_w"].T.astype(F32), ssd_cb=row(p["ssd_conv_b"]),
        ssd_hp=pad_rows([_lane_row([(SM_SSD_DT, p["ssd_A_log"])]),
                         _lane_row([(SM_SSD_DT, p["ssd_dt_bias"])])]),
        ssd_norm=row(p["ssd_norm"]),
        ssd_sel_p=_spread_matrix([(SM_SSD_DT + h, SSD_P) for h in range(SSD_HEADS)]),
        ssd_sel_n=_spread_matrix([(SM_SSD_DT + h, SSD_N) for h in range(SSD_HEADS)]),
        ssd_d=row(jnp.repeat(p["ssd_D"], SSD_P)),
        w_br_gdn=mats["w_br_gdn"], w_br_gla=mats["w_br_gla"], w_br_ssd=mats["w_br_ssd"],
        w_out=mats["w_out"])


def _layer(x, state, w, wl, bsz, t, n, n_valid, layer=None, s_acc=(None, None, None)):
    gdn_hist, s_gdn, s_gla, ssd_hist, s_ssd = state
    x = _ffn(x, *w["ffn1"], wl)
    fuse_conv = t > n and t % min(INPROJ_ROW_TILE, bsz * t) == 0
    conv = (t, gdn_hist, ssd_hist, w["gdn_cw"], w["ssd_cw"], w["ssd_cb"]) if fuse_conv else None
    outs = _inproj(x, w["mix_norm_pre"], w["w_in"], wl, conv)
    proj = dict(zip((name for name, _ in PROJ_GROUPS), outs))
    seq = lambda a: a.reshape(bsz, t, a.shape[-1])
    sm = seq(proj["small"])
    t_valid = t - (n - n_valid)
    if fuse_conv:
        gdn_hist = ssd_hist = None
        tails = [a[:, SUBLANES - (CONV_WIDTH - 1):] for a in outs[len(PROJ_GROUPS):]]
    else:
        tails = [seq(proj[g])[:, t_valid - (CONV_WIDTH - 1):t_valid] for g in CONV_GROUPS]
    o_gdn, s_gdn_new = _scan_call(
        "gdn", _gdn_core, [seq(proj["gdn_qkv"]), seq(proj["gdn_z"]), sm], 0, gdn_hist, s_gdn,
        [w["gdn_cw"], w["gdn_hp"], w["gdn_norm"], w["gdn_sel"]], GDN_V, n, n_valid, layer, s_acc[0])
    o_gla, s_gla_new = _scan_call(
        "gla", _gla_core, [seq(proj["gla_qk"]), seq(proj["gla_v"]), seq(proj["gla_g"]), sm], None,
        None, s_gla, [w["gla_wup"], w["gla_gb"], w["gla_norm"]], GLA_V, n, n_valid, layer, s_acc[1])
    o_ssd, s_ssd_new = _scan_call(
        "ssd", _ssd_core, [seq(proj["ssd_z"]), seq(proj["ssd_xbc"]), sm], 1, ssd_hist, s_ssd,
        [w["ssd_cw"], w["ssd_cb"], w["ssd_hp"], w["ssd_norm"], w["ssd_sel_p"], w["ssd_sel_n"],
         w["ssd_d"]], SSD_INNER, n, n_valid, layer, s_acc[2])
    flat = lambda a: a.reshape(bsz * t, a.shape[-1])
    x = _merge(x, flat(o_gdn), flat(o_gla), flat(o_ssd), w["mix_norm_pre"], w["w_gate"], w["b_merge"],
               w["w_br_gdn"], w["w_br_gla"], w["w_br_ssd"], w["w_out"], w["mix_norm_post"], wl)
    x = _ffn(x, *w["ffn2"], wl)
    return x, (tails[0], s_gdn_new, s_gla_new, tails[1], s_ssd_new)


def kernel(x_prompt, x_sample, state_gdn_conv, state_gdn, state_gla, state_ssd_conv, state_ssd,
           ffn1_norm_pre, ffn1_norm_post, ffn1_w_gate, ffn1_w_up, ffn1_w_down,
           mix_norm_pre, mix_norm_post, w_in, b_merge,
           gdn_conv_w, gdn_A_log, gdn_dt_bias, gdn_norm,
           gla_gate_up, gla_gate_bias, gla_norm,
           ssd_conv_w, ssd_conv_b, ssd_A_log, ssd_dt_bias, ssd_D, ssd_norm,
           w_br_gdn, w_br_gla, w_br_ssd, w_out,
           ffn2_norm_pre, ffn2_norm_post, ffn2_w_gate, ffn2_w_up, ffn2_w_down):
    params = dict(
        ffn1_norm_pre=ffn1_norm_pre, ffn1_norm_post=ffn1_norm_post, ffn1_w_gate=ffn1_w_gate,
        ffn1_w_up=ffn1_w_up, ffn1_w_down=ffn1_w_down,
        mix_norm_pre=mix_norm_pre, mix_norm_post=mix_norm_post, w_in=w_in, b_merge=b_merge,
        gdn_conv_w=gdn_conv_w, gdn_A_log=gdn_A_log, gdn_dt_bias=gdn_dt_bias, gdn_norm=gdn_norm,
        gla_gate_up=gla_gate_up, gla_gate_bias=gla_gate_bias, gla_norm=gla_norm,
        ssd_conv_w=ssd_conv_w, ssd_conv_b=ssd_conv_b, ssd_A_log=ssd_A_log, ssd_dt_bias=ssd_dt_bias,
        ssd_D=ssd_D, ssd_norm=ssd_norm,
        w_br_gdn=w_br_gdn, w_br_gla=w_br_gla, w_br_ssd=w_br_ssd, w_out=w_out,
        ffn2_norm_pre=ffn2_norm_pre, ffn2_norm_post=ffn2_norm_post, ffn2_w_gate=ffn2_w_gate,
        ffn2_w_up=ffn2_w_up, ffn2_w_down=ffn2_w_down)
    bp, tp, _ = x_prompt.shape
    bs, ts, _ = x_sample.shape
    ts_pad = SUBLANES
    n_p = min(CHUNK, tp)
    depth = state_gdn.shape[0]

    prompt_init = (jnp.zeros((bp, SUBLANES, GDN_CONV), F32),
                   jnp.zeros((bp, GDN_HEADS, GDN_DK, GDN_DV), F32),
                   jnp.zeros((bp, GLA_HEADS, GLA_DK, GLA_DV), F32),
                   jnp.zeros((bp, SUBLANES, SSD_CONV), F32),
                   jnp.zeros((bp, SSD_HEADS, SSD_P, SSD_N), F32))
    hist_rows = lambda s: jnp.pad(s.astype(F32), ((0, 0), (SUBLANES - (CONV_WIDTH - 1), 0), (0, 0)))
    sample_states = tuple(s.astype(F32) for s in (state_gdn, state_gla, state_ssd))
    s_acc = tuple(jnp.zeros(s.shape, F32) for s in sample_states)

    y_p = x_prompt.reshape(bp * tp, D_MODEL)
    y_s = jnp.pad(x_sample, ((0, 0), (0, ts_pad - ts), (0, 0))).reshape(bs * ts_pad, D_MODEL)
    new_p, s_conv = [], []
    mats = _prep_matrices(params)
    small = {name: arr for name, arr in params.items() if name not in mats and name != "w_in"}
    for l in range(depth):
        w = _prep_layer({name: arr[l] for name, arr in small.items()}, mats)
        y_p, st_p = _layer(y_p, prompt_init, w, l, bp, tp, n_p, n_p)
        sample_state = (hist_rows(state_gdn_conv[l]), sample_states[0], sample_states[1],
                        hist_rows(state_ssd_conv[l]), sample_states[2])
        y_s, st_s = _layer(y_s, sample_state, w, l, bs, ts_pad, ts_pad, ts, layer=l, s_acc=s_acc)
        new_p.append(st_p)
        s_conv.append((st_s[0], st_s[3]))
        s_acc = (st_s[1], st_s[2], st_s[4])
    p_out = tuple(jnp.stack(a) for a in zip(*new_p))
    s_gdn_conv, s_ssd_conv = (jnp.stack(a) for a in zip(*s_conv))
    y_prompt = y_p.reshape(bp, tp, D_MODEL)
    y_sample = y_s.reshape(bs, ts_pad, D_MODEL)[:, :ts]
    return (y_prompt, y_sample) + p_out + (s_gdn_conv, s_acc[0], s_acc[1], s_ssd_conv, s_acc[2])
```

```python
import functools

import jax
import jax.numpy as jnp
import numpy as np
from jax import lax
from jax.experimental import pallas as pl
from jax.experimental.pallas import tpu as pltpu

F32 = jnp.float32
BF16 = jnp.bfloat16

D_MODEL = 1024
DEPTH = 2
D_FF = 2816
MXU_WIDTH = 256
NORM_EPS = 1e-6
CONV_WIDTH = 4
CHUNK = 64
SUBLANES = 8

GDN_HEADS, GDN_DK, GDN_DV = 4, 128, 128
GLA_HEADS, GLA_DK, GLA_DV = 4, 64, 128
GLA_RANK = 16
GLA_GATE_NORMALIZER = 16.0
GLA_SUB = 16
SSD_HEADS, SSD_P, SSD_N, SSD_GROUPS = 8, 64, 128, 2

GDN_QK = GDN_HEADS * GDN_DK
GDN_V = GDN_HEADS * GDN_DV
GDN_CONV = 2 * GDN_QK + GDN_V
GLA_QK = GLA_HEADS * GLA_DK
GLA_V = GLA_HEADS * GLA_DV
SSD_INNER = SSD_HEADS * SSD_P
SSD_BC = SSD_GROUPS * SSD_N
SSD_CONV = SSD_INNER + 2 * SSD_BC
N_GATES = 3 * D_MODEL
SMALL_W = 128
SM_GDN_A, SM_GDN_B, SM_GLA_R, SM_SSD_DT = 0, 4, 8, 24

PROJ_GROUPS = (("gdn_qkv", GDN_CONV), ("ssd_xbc", SSD_CONV), ("gdn_z", GDN_V), ("gla_qk", 2 * GLA_QK),
               ("gla_v", GLA_V), ("gla_g", GLA_V), ("ssd_z", SSD_INNER), ("small", SMALL_W))
D_PROJ = sum(w for _, w in PROJ_GROUPS)
CONV_GROUPS = ("gdn_qkv", "ssd_xbc")

VMEM_LIMIT = 56 * 1024 * 1024
ROW_TILE = 512
INPROJ_ROW_TILE = 512
PROMPT_CHUNKS_PER_STEP = dict(gdn=4, gla=8, ssd=4)
DECODE_SEQS_PER_STEP = 8
DECODE_SEQS_TOGETHER = 8


def _cparams(n_axes):
    return pltpu.CompilerParams(dimension_semantics=("arbitrary",) * n_axes,
                                vmem_limit_bytes=VMEM_LIMIT)


def _mm(a, b):
    return jnp.dot(a.astype(BF16), b.astype(BF16), preferred_element_type=F32)


def _mm_nt(a, b):
    return lax.dot_general(a.astype(BF16), b.astype(BF16), (((1,), (1,)), ((), ())),
                           preferred_element_type=F32)


def _mm_tn(a, b):
    return lax.dot_general(a, b, (((0,), (0,)), ((), ())), preferred_element_type=F32)


def _split2(x):
    hi = x.astype(BF16).astype(F32)
    return hi, x - hi


def _split3_lanes(x):
    hi = x.astype(BF16)
    r = x - hi.astype(F32)
    mid = r.astype(BF16)
    lo = (r - mid.astype(F32)).astype(BF16)
    return jnp.concatenate([hi, mid, lo], axis=1)


def _mm_x3(a2, b2):
    (ah, al), (bh, bl) = a2, b2
    lhs = jnp.concatenate([ah, al, ah], axis=1).astype(BF16)
    rhs = jnp.concatenate([bh, bh, bl], axis=0).astype(BF16)
    return jnp.dot(lhs, rhs, preferred_element_type=F32)


def _cumsum_rows(tri, x):
    w = x.shape[1]
    y = jnp.dot(tri, _split3_lanes(x), preferred_element_type=F32)
    return y[:, :w] + y[:, w:2 * w] + y[:, 2 * w:]


def _lane_spread(x, sel3_ref):
    return jnp.dot(_split3_lanes(x), sel3_ref[...], preferred_element_type=F32)


def _rms(x):
    return x * lax.rsqrt(jnp.mean(x * x, axis=-1, keepdims=True) + NORM_EPS)


def _l2n(x):
    return x * lax.rsqrt(jnp.sum(x * x, axis=-1, keepdims=True) + NORM_EPS)


def _silu(x):
    return x * jax.nn.sigmoid(x)


def _softplus(x):
    return jnp.maximum(x, 0.0) + jnp.log1p(jnp.exp(-jnp.abs(x)))


def _log_sigmoid(x):
    return jnp.minimum(x, 0.0) - jnp.log1p(jnp.exp(-jnp.abs(x)))


def _iota2(n, m, dim):
    return lax.broadcasted_iota(jnp.int32, (n, m), dim)


def _masked_decay(diff, mask):
    return jnp.where(mask, jnp.exp(jnp.where(mask, diff, 0.0)), 0.0)


def _rows_as_lanes(x, n_rows):
    if n_rows < 128:
        x = jnp.concatenate([x, jnp.zeros((128 - n_rows, 128), F32)], axis=0)
    return x.T


def _conv_taps(xp, cw_ref, n_rows):
    x = xp[0:SUBLANES + n_rows, :]
    z = x * cw_ref[0:1, :]
    for i in range(1, CONV_WIDTH):
        z = pltpu.roll(z, 1, axis=0) + x * cw_ref[i:i + 1, :]
    return z[SUBLANES:, :]


def _conv_rows(xp, x_ref, cw_ref, n_rows):
    xp[SUBLANES:SUBLANES + n_rows, :] = x_ref[...]
    return _conv_taps(xp, cw_ref, n_rows)


def _each(fn, *lists):
    return [fn(*args) for args in zip(*lists)]


def _unit_lower_solve(a, rhs, n, n_valid):
    if n == CHUNK:
        r, c = _iota2(n, n, 0), _iota2(n, n, 1)
        eye = (r == c).astype(F32)
        same_block = lax.shift_right_logical(r, 4) == lax.shift_right_logical(c, 4)
        sp = lambda xs: [_split2(x) for x in xs]
        times = lambda xs, ys: _each(_mm_x3, xs, ys)
        plus = lambda xs, ys: _each(lambda x, y: x + y, xs, ys)
        d = _each(lambda x: jnp.where(same_block, x, 0.0), a)
        off = _each(lambda x, y: x - y, a, d)
        d_s = sp(d)
        d2_s = sp(times(d_s, d_s))
        t = _each(lambda x: eye - x, d)
        t = plus(t, times(sp(t), d2_s))
        d4_s = sp(times(d2_s, d2_s))
        t = plus(t, times(sp(t), d4_s))
        d8_s = sp(times(d4_s, d4_s))
        t_s = sp(plus(t, times(sp(t), d8_s)))
        b_s = sp(times(t_s, sp(off)))
        y = times(t_s, sp(rhs))
        b2_s = sp(times(b_s, b_s))
        z = _each(lambda yy, by: yy - by, y, times(b_s, sp(y)))
        return plus(z, times(b2_s, sp(z)))
    sols = []
    for ah, rh in zip(a, rhs):
        rows = [rh[0:1]]
        for i in range(1, n_valid):
            x = rh[i:i + 1]
            for j in range(i):
                x = x - ah[i:i + 1, j:j + 1] * rows[j]
            rows.append(x)
        if n_valid < n:
            rows.append(rh[n_valid:n])
        sols.append(jnp.concatenate(rows, axis=0))
    return sols


def _ffn_kernel(x_ref, npre_ref, wg_ref, wu_ref, wd_ref, npost_ref, o_ref, h_ref, a_ref, f_ref):
    x = x_ref[...]
    h_ref[...] = (_rms(x) * npre_ref[...]).astype(BF16)
    for c in range(0, D_FF, MXU_WIDTH):
        cols = slice(c, c + MXU_WIDTH)
        g = jnp.dot(h_ref[...], wg_ref[:, cols], preferred_element_type=F32)
        u = jnp.dot(h_ref[...], wu_ref[:, cols], preferred_element_type=F32)
        a_ref[:, cols] = (_silu(g) * u).astype(BF16)
    for c in range(0, D_MODEL, MXU_WIDTH):
        cols = slice(c, c + MXU_WIDTH)
        f_ref[:, cols] = jnp.dot(a_ref[...], wd_ref[:, cols], preferred_element_type=F32)
    o_ref[...] = x + 0.5 * (_rms(f_ref[...]) * npost_ref[...])


def _of_layer(a, layer):
    return pl.BlockSpec((None,) + a.shape[1:], lambda *_: (layer, 0, 0))


def _ffn(x, npre, wg, wu, wd, npost, layer):
    n = x.shape[0]
    tm = min(ROW_TILE, n)
    const = lambda i: (0, 0)
    return pl.pallas_call(
        _ffn_kernel,
        grid=(n // tm,),
        in_specs=[pl.BlockSpec((tm, D_MODEL), lambda i: (i, 0)),
                  pl.BlockSpec((1, D_MODEL), const),
                  _of_layer(wg, layer), _of_layer(wu, layer), _of_layer(wd, layer),
                  pl.BlockSpec((1, D_MODEL), const)],
        out_specs=pl.BlockSpec((tm, D_MODEL), lambda i: (i, 0)),
        out_shape=jax.ShapeDtypeStruct((n, D_MODEL), F32),
        scratch_shapes=[pltpu.VMEM((tm, D_MODEL), BF16), pltpu.VMEM((tm, D_FF), BF16),
                        pltpu.VMEM((tm, D_MODEL), F32)],
        compiler_params=_cparams(1),
        name=f"ffn_{n}",
    )(x, npre, wg, wu, wd, npost)


def _inproj_kernel(*refs, conv, tiles_per_seq, tm):
    x_ref, n_ref, w_ref = refs[:3]
    refs = refs[3:]
    if conv:
        hist = dict(zip(CONV_GROUPS, refs[:2]))
        cw = dict(zip(CONV_GROUPS, refs[2:4]))
        cb_ref = refs[4]
        refs = refs[5:]
    o_refs = dict(zip((name for name, _ in PROJ_GROUPS), refs[:len(PROJ_GROUPS)]))
    refs = refs[len(PROJ_GROUPS):]
    h_ref = refs[-3] if conv else refs[-1]
    if conv:
        tail = dict(zip(CONV_GROUPS, refs[:2]))
        pre = dict(zip(CONV_GROUPS, refs[-2:]))

        @pl.when(lax.rem(pl.program_id(0), tiles_per_seq) == 0)
        def _():
            for name in CONV_GROUPS:
                pre[name][0:SUBLANES, :] = hist[name][...]

    h_ref[...] = (_rms(x_ref[...]) * n_ref[...]).astype(BF16)
    col = 0
    for name, width in PROJ_GROUPS:
        step = min(width, 512)
        for s in range(0, width, step):
            y = jnp.dot(h_ref[...], w_ref[:, col + s:col + s + step], preferred_element_type=F32)
            if conv and name in CONV_GROUPS:
                pre[name][SUBLANES:SUBLANES + tm, s:s + step] = y
            else:
                o_refs[name][:, s:s + step] = y
        col += width
    if not conv:
        return
    for name in CONV_GROUPS:
        xp = pre[name]
        y = _conv_taps(xp, cw[name], tm)
        o_refs[name][...] = _silu(y + cb_ref[...] if name == "ssd_xbc" else y)
        tail[name][...] = xp[tm:tm + SUBLANES, :]
        xp[0:SUBLANES, :] = xp[tm:tm + SUBLANES, :]


def _inproj(x, norm, w, layer, conv=None):
    n = x.shape[0]
    tm = min(INPROJ_ROW_TILE, n)
    const = lambda a: pl.BlockSpec(a.shape, lambda i: (0,) * a.ndim)
    operands = [x, norm, w]
    in_specs = [pl.BlockSpec((tm, D_MODEL), lambda i: (i, 0)), const(norm), _of_layer(w, layer)]
    out_specs = [pl.BlockSpec((tm, wd), lambda i: (i, 0)) for _, wd in PROJ_GROUPS]
    out_shape = [jax.ShapeDtypeStruct((n, wd), F32) for _, wd in PROJ_GROUPS]
    scratch, tiles_per_seq = [pltpu.VMEM((tm, D_MODEL), BF16)], 1
    if conv is not None:
        t, hist_g, hist_s, cw_g, cw_s, cb_s = conv
        tiles_per_seq = t // tm
        per_seq = lambda a: pl.BlockSpec((None,) + a.shape[1:], lambda i: (i // tiles_per_seq, 0, 0))
        operands += [hist_g, hist_s, cw_g, cw_s, cb_s]
        in_specs += [per_seq(hist_g), per_seq(hist_s), const(cw_g), const(cw_s), const(cb_s)]
        out_specs += [per_seq(hist_g), per_seq(hist_s)]
        out_shape += [jax.ShapeDtypeStruct(hist_g.shape, F32), jax.ShapeDtypeStruct(hist_s.shape, F32)]
        scratch += [pltpu.VMEM((SUBLANES + tm, GDN_CONV), F32), pltpu.VMEM((SUBLANES + tm, SSD_CONV), F32)]
    return pl.pallas_call(
        functools.partial(_inproj_kernel, conv=conv is not None, tiles_per_seq=tiles_per_seq, tm=tm),
        grid=(n // tm,),
        in_specs=in_specs,
        out_specs=out_specs,
        out_shape=out_shape,
        scratch_shapes=scratch,
        compiler_params=_cparams(1),
        name=f"inproj_{n}",
    )(*operands)


def _merge_kernel(x_ref, og_ref, ol_ref, os_ref, npre_ref, wgate_ref, bm_ref, wg_ref, wl_ref, ws_ref,
                  wo_ref, npost_ref, o_ref):
    x = x_ref[...]
    h = (_rms(x) * npre_ref[...]).astype(BF16)
    merged = None
    for i, (b_ref, w_ref) in enumerate(((og_ref, wg_ref), (ol_ref, wl_ref), (os_ref, ws_ref))):
        logits = jnp.dot(h, wgate_ref[:, i * D_MODEL:(i + 1) * D_MODEL], preferred_element_type=F32)
        term = jax.nn.sigmoid(logits + bm_ref[i:i + 1, :]) * _mm(b_ref[...], w_ref[...])
        merged = term if merged is None else merged + term
    out = _mm(merged, wo_ref[...])
    o_ref[...] = x + _rms(out) * npost_ref[...]


def _merge(x, o_gdn, o_gla, o_ssd, npre, w_gate, b_merge, w_gdn, w_gla, w_ssd, w_out, npost, layer):
    n = x.shape[0]
    tm = min(ROW_TILE, n)
    row = lambda w: pl.BlockSpec((tm, w), lambda i: (i, 0))
    full = lambda a: _of_layer(a, layer) if a.ndim == 3 else pl.BlockSpec(a.shape, lambda i: (0, 0))
    consts = (npre, w_gate, b_merge, w_gdn, w_gla, w_ssd, w_out, npost)
    return pl.pallas_call(
        _merge_kernel,
        grid=(n // tm,),
        in_specs=[row(D_MODEL), row(GDN_V), row(GLA_V), row(SSD_INNER)] + [full(a) for a in consts],
        out_specs=row(D_MODEL),
        out_shape=jax.ShapeDtypeStruct((n, D_MODEL), F32),
        compiler_params=_cparams(1),
        name=f"merge_{n}",
    )(x, o_gdn, o_gla, o_ssd, *consts)


def _gdn_core(acts, conv, s_in, params, o_views, *, n, n_valid, chained, activated):
    _, z_views, sm_views = acts
    _, hp_ref, nrm_ref, sel3_ref = params
    m, nh = len(conv), GDN_HEADS
    valid = _iota2(n, 1, 0) < n_valid
    pad = (lambda t: jnp.where(valid, t, 0.0)) if n_valid < n else (lambda t: t)
    r, c = _iota2(n, n, 0), _iota2(n, n, 1)
    causal, strict = r >= c, r > c
    lane = _iota2(n, SMALL_W, 1)

    qkv = conv if activated else [_silu(y) for y in conv]
    sm = [v[...] for v in sm_views]
    g_all = [pad(-jnp.exp(hp_ref[0:1, :]) * _softplus(x + hp_ref[1:2, :])) for x in sm]
    tri = causal.astype(F32).astype(BF16)
    gcum = [_cumsum_rows(tri, g) for g in g_all]
    gcum_t = [_rows_as_lanes(g, n) for g in gcum]
    units = [(j, h) for j in range(m) for h in range(nh)]
    beta_all = [pad(jax.nn.sigmoid(x)) for x in sm]
    if n == CHUNK:
        spread = [_lane_spread(jnp.where(lane < SM_GDN_B, g, b), sel3_ref)
                  for g, b in zip(gcum, beta_all)]
        gc = [spread[j][:, h * GDN_DV:(h + 1) * GDN_DV] for j, h in units]
        beta = [spread[j][:, GDN_V + h * GDN_DV:GDN_V + (h + 1) * GDN_DV] for j, h in units]
    else:
        wide = lambda col: jnp.broadcast_to(col, (n, GDN_DV))
        gc = [wide(gcum[j][:, SM_GDN_A + h:SM_GDN_A + h + 1]) for j, h in units]
        beta = [wide(beta_all[j][:, SM_GDN_B + h:SM_GDN_B + h + 1]) for j, h in units]
    gr = [gcum_t[j][SM_GDN_A + h:SM_GDN_A + h + 1, 0:n] for j, h in units]
    q = [pad(_l2n(qkv[j][:, h * GDN_DK:(h + 1) * GDN_DK]) * (GDN_DK ** -0.5)) for j, h in units]
    k = [pad(_l2n(qkv[j][:, GDN_QK + h * GDN_DK:GDN_QK + (h + 1) * GDN_DK])) for j, h in units]
    v = [pad(qkv[j][:, 2 * GDN_QK + h * GDN_DV:2 * GDN_QK + (h + 1) * GDN_DV]) for j, h in units]
    dec = _each(lambda x, y: _masked_decay(x[:, :n] - y, causal), gc, gr)
    eg = _each(jnp.exp, gc)
    kk = _each(_mm_nt, k, k)
    qk = _each(_mm_nt, q, k)
    a = _each(lambda b, x, d: jnp.where(strict, b[:, :n] * x * d, 0.0), beta, kk, dec)
    rhs = _each(lambda b, e, vv, kx: jnp.concatenate([b * vv, (b * e) * kx], axis=-1), beta, eg, v, k)
    sol = _unit_lower_solve(a, rhs, n, n_valid)
    att = _each(lambda x, d: x * d, qk, dec)
    q_in = _each(lambda x, e: x * e, q, eg)
    g_last = [x[n - 1:n, :] for x in gc]
    k_out = _each(lambda kx, gl, x: kx * jnp.exp(gl - x), k, g_last, gc)
    e_last = _each(jnp.exp, g_last)

    def advance(idx, s):
        u = [sol[i][:, :GDN_DV] - _mm(sol[i][:, GDN_DV:], ss) for i, ss in zip(idx, s)]
        o = [_mm(q_in[i], ss) + _mm(att[i], uu) for i, ss, uu in zip(idx, s, u)]
        s_new = [e_last[i] * ss + _mm_tn(k_out[i], uu) for i, ss, uu in zip(idx, s, u)]
        for i, oo in zip(idx, o):
            j, h = units[i]
            cols = slice(h * GDN_DV, (h + 1) * GDN_DV)
            o_views[j][:, cols] = (_rms(oo) * nrm_ref[...]) * _silu(z_views[j][:, cols])
        return s_new

    if chained:
        s = s_in
        for j in range(m):
            s = advance(list(range(j * nh, (j + 1) * nh)), s)
        return s
    s_new = advance(list(range(m * nh)), [s_in[j][h] for j, h in units])
    return [s_new[j * nh:(j + 1) * nh] for j in range(m)]


def _gla_core(acts, conv, s_in, params, o_views, *, n, n_valid, chained, activated):
    qk_views, v_views, g_views, sm_views = acts
    wup_ref, gb_ref, nrm_ref = params
    m, nh = len(qk_views), GLA_HEADS
    rows = _iota2(n, 1, 0)
    valid = rows < n_valid
    pad = (lambda t: jnp.where(valid, t, 0.0)) if n_valid < n else (lambda t: t)
    r, c = _iota2(n, n, 0), _iota2(n, n, 1)
    causal = r >= c
    segs = list(range(m))

    gk = [pad(_log_sigmoid(_mm(sm_views[j][...], wup_ref[...]) + gb_ref[...])
              * (1.0 / GLA_GATE_NORMALIZER)) for j in segs]
    q = [pad(qk_views[j][:, :GLA_QK] * (GLA_DK ** -0.5)) for j in segs]
    k = [pad(qk_views[j][:, GLA_QK:]) for j in segs]
    v = [pad(v_views[j][...]) for j in segs]
    tri = causal.astype(F32).astype(BF16)
    bc = [_cumsum_rows(tri, x) for x in gk]

    units = [(j, h) for j in segs for h in range(nh)]
    hs = [slice(h * GLA_DK, (h + 1) * GLA_DK) for h in range(nh)]
    vs = [slice(h * GLA_DV, (h + 1) * GLA_DV) for h in range(nh)]
    sub = min(GLA_SUB, n)
    att_rows = [[] for _ in units]
    for blk in range(n // sub):
        lo, hi = blk * sub, (blk + 1) * sub
        seen = rows < hi
        for j in segs:
            ref_row = bc[j][lo - 1:lo, :] if blk > 0 else jnp.zeros((1, GLA_QK), F32)
            qa = q[j][lo:hi] * jnp.exp(bc[j][lo:hi] - ref_row)
            ka = jnp.where(seen, k[j] * jnp.exp(jnp.where(seen, ref_row - bc[j], 0.0)), 0.0)
            for h in range(nh):
                att_rows[j * nh + h].append(_mm_nt(qa[:, hs[h]], ka[:, hs[h]]))
    att = [jnp.where(causal, x[0] if len(x) == 1 else jnp.concatenate(x, axis=0), 0.0)
           for x in att_rows]

    b_last = [x[n - 1:n, :] for x in bc]
    q_in = _each(lambda x, b: x * jnp.exp(b), q, bc)
    k_out = _each(lambda x, bl, b: x * jnp.exp(bl - b), k, b_last, bc)
    e_last = _each(jnp.exp, b_last)
    dpair = [[jnp.broadcast_to(e[:, p * 128:(p + 1) * 128], (128, 128)).T for p in range(nh // 2)]
             for e in e_last]
    dcol = [dpair[j][h // 2][(h % 2) * GLA_DK:(h % 2 + 1) * GLA_DK, :] for j, h in units]
    o_att = [_mm(att[i], v[j][:, vs[h]]) for i, (j, h) in enumerate(units)]
    upd = [_mm_tn(k_out[j][:, hs[h]], v[j][:, vs[h]]) for j, h in units]

    def emit(idx, s):
        o = [_mm(q_in[units[i][0]][:, hs[units[i][1]]], ss) + o_att[i] for i, ss in zip(idx, s)]
        for i, oo in zip(idx, o):
            j, h = units[i]
            o_views[j][:, vs[h]] = (_rms(oo) * nrm_ref[...]) * _silu(g_views[j][:, vs[h]])

    step = lambda idx, s: [dcol[i] * ss + upd[i] for i, ss in zip(idx, s)]
    everything = list(range(m * nh))
    if chained:
        starts, s = [], s_in
        for j in segs:
            starts += s
            s = step(list(range(j * nh, (j + 1) * nh)), s)
        emit(everything, starts)
        return s
    starts = [s_in[j][h] for j, h in units]
    emit(everything, starts)
    s_new = step(everything, starts)
    return [s_new[j * nh:(j + 1) * nh] for j in segs]


def _ssd_core(acts, conv, s_in, params, o_views, *, n, n_valid, chained, activated):
    z_views, _, sm_views = acts
    _, cb_ref, hp_ref, nrm_ref, sel_p3_ref, sel_n3_ref, d_ref = params
    m, nh = len(conv), SSD_HEADS
    heads_per_group = nh // SSD_GROUPS
    group_w = SSD_INNER // SSD_GROUPS
    valid = _iota2(n, 1, 0) < n_valid
    pad = (lambda t: jnp.where(valid, t, 0.0)) if n_valid < n else (lambda t: t)
    r, c = _iota2(n, n, 0), _iota2(n, n, 1)
    causal = r >= c
    segs = list(range(m))

    xbc = conv if activated else [_silu(y + cb_ref[...]) for y in conv]
    dt_all = [pad(_softplus(sm_views[j][...] + hp_ref[1:2, :])) for j in segs]
    tri = causal.astype(F32).astype(BF16)
    acum = [_cumsum_rows(tri, -jnp.exp(hp_ref[0:1, :]) * x) for x in dt_all]
    acum_t = [_rows_as_lanes(x, n) for x in acum]
    dt_p = [_lane_spread(x, sel_p3_ref) for x in dt_all]
    ac_p = [_lane_spread(x, sel_p3_ref) for x in acum]
    ac_n = [_lane_spread(x, sel_n3_ref) for x in acum]
    xs = [x[:, :SSD_INNER] for x in xbc]
    xdt_all = _each(lambda x, d: x * d, xs, dt_p)
    x_out_all = _each(lambda x, a: x * jnp.exp(a[n - 1:n, :] - a), xdt_all, ac_p)
    skip_all = [d_ref[...] * x for x in xs]
    e_n = _each(jnp.exp, ac_n)
    pairs = [(j, g) for j in segs for g in range(SSD_GROUPS)]
    bm = [pad(xbc[j][:, SSD_INNER + g * SSD_N:SSD_INNER + (g + 1) * SSD_N]) for j, g in pairs]
    cm = [xbc[j][:, SSD_INNER + SSD_BC + g * SSD_N:SSD_INNER + SSD_BC + (g + 1) * SSD_N]
          for j, g in pairs]
    cb = _each(_mm_nt, cm, bm)

    units = [(j, h) for j in segs for h in range(nh)]
    grp = [j * SSD_GROUPS + h // heads_per_group for j, h in units]
    pcols = [slice(h * SSD_P, (h + 1) * SSD_P) for _, h in units]
    ncols = [slice(h * SSD_N, (h + 1) * SSD_N) for _, h in units]
    ar = [acum_t[j][SM_SSD_DT + h:SM_SSD_DT + h + 1, 0:n] for j, h in units]
    dec = [_masked_decay(ac_p[j][:, cs][:, :n] - rr, causal)
           for (j, _), cs, rr in zip(units, pcols, ar)]
    y_att = [_mm(cb[g] * d, xdt_all[j][:, cs]) for g, d, (j, _), cs in zip(grp, dec, units, pcols)]
    c_in = [cm[g] * e_n[j][:, cs] for g, (j, _), cs in zip(grp, units, ncols)]
    upd = [_mm_tn(x_out_all[j][:, cs], bm[g]) for g, (j, _), cs in zip(grp, units, pcols)]
    e_last = [e_n[j][n - 1:n, cs] for (j, _), cs in zip(units, ncols)]
    skip = [skip_all[j][:, cs] for (j, _), cs in zip(units, pcols)]

    def emit(idx, s):
        y = [y_att[i] + _mm_nt(c_in[i], ss) + skip[i] for i, ss in zip(idx, s)]
        y = [yy * _silu(z_views[units[i][0]][:, units[i][1] * SSD_P:(units[i][1] + 1) * SSD_P])
             for i, yy in zip(idx, y)]
        for p in sorted({grp[i] for i in idx}):
            mine = [(i, yy) for i, yy in zip(idx, y) if grp[i] == p]
            ms = sum(jnp.sum(yy * yy, axis=-1, keepdims=True) for _, yy in mine) * (1.0 / group_w)
            scale = lax.rsqrt(ms + NORM_EPS)
            for i, yy in mine:
                j, h = units[i]
                cols = slice(h * SSD_P, (h + 1) * SSD_P)
                o_views[j][:, cols] = (yy * scale) * nrm_ref[:, cols]

    step = lambda idx, s: [e_last[i] * ss + upd[i] for i, ss in zip(idx, s)]
    everything = list(range(m * nh))
    if chained:
        starts, s = [], s_in
        for j in segs:
            starts += s
            s = step(list(range(j * nh, (j + 1) * nh)), s)
        emit(everything, starts)
        return s
    starts = [s_in[j][h] for j, h in units]
    emit(everything, starts)
    s_new = step(everything, starts)
    return [s_new[j * nh:(j + 1) * nh] for j in segs]


def _scan_kernel(core, n_act, conv_idx, has_conv, n_param, n_heads, n, m, nb, aliased):
    chained = nb is None
    activated = conv_idx is not None and not has_conv
    core = functools.partial(core, activated=activated)

    def kern(*refs):
        it = iter(refs)
        acts = [next(it) for _ in range(n_act)]
        hist = next(it) if has_conv else None
        s0 = next(it)
        if aliased:
            next(it)
        params = [next(it) for _ in range(n_param)]
        o_ref, s_ref = next(it), next(it)
        xp_ref = next(it) if has_conv else None

        if chained:
            @pl.when(pl.program_id(1) == 0)
            def _():
                s_ref[...] = s0[...]
                if has_conv:
                    xp_ref[0:SUBLANES, :] = hist[...]

            rows = lambda ref: [ref.at[pl.ds(j * n, n)] for j in range(m)]
            conv = None
            if has_conv:
                y = _conv_rows(xp_ref, acts[conv_idx], params[0], m * n)
                xp_ref[0:SUBLANES, :] = xp_ref[m * n:m * n + SUBLANES, :]
                conv = [y[j * n:(j + 1) * n] for j in range(m)]
            elif activated:
                conv = [v[...] for v in rows(acts[conv_idx])]
            s_out = core([rows(a) for a in acts], conv, [s_ref[h] for h in range(n_heads)], params,
                         rows(o_ref), chained=True)
            for h in range(n_heads):
                s_ref[h] = s_out[h]
            return

        def step(i, carry):
            pick = lambda ref: [ref.at[i * m + j] for j in range(m)]
            conv = None
            if has_conv:
                conv = []
                for j, (hv, xv) in enumerate(zip(pick(hist), pick(acts[conv_idx]))):
                    xp = xp_ref.at[j]
                    xp[0:SUBLANES, :] = hv[...]
                    conv.append(_conv_rows(xp, xv, params[0], n))
            elif activated:
                conv = [v[...] for v in pick(acts[conv_idx])]
            s_in = [[sv[h] for h in range(n_heads)] for sv in pick(s0)]
            s_out = core([pick(a) for a in acts], conv, s_in, params, pick(o_ref), chained=False)
            for sv, so in zip(pick(s_ref), s_out):
                for h in range(n_heads):
                    sv[h] = so[h]
            return carry

        lax.fori_loop(0, nb // m, step, 0)

    return kern


def _scan_call(name, core, acts, conv_idx, hist, s0, params, out_w, n, n_valid, layer, s_acc):
    bsz, t, _ = acts[0].shape
    nc = t // n
    state_shape = s0.shape[-3:]
    aliased = s_acc is not None
    if nc > 1:
        m = PROMPT_CHUNKS_PER_STEP[name] if nc % PROMPT_CHUNKS_PER_STEP[name] == 0 else 1
        nb, lead, grid = None, None, (bsz, nc // m)
    else:
        nb = DECODE_SEQS_PER_STEP if bsz % DECODE_SEQS_PER_STEP == 0 else 1
        m = DECODE_SEQS_TOGETHER if nb % DECODE_SEQS_TOGETHER == 0 else 1
        lead, grid = nb, (bsz // nb, 1)
    rows = n * m if nb is None else n

    act_spec = lambda w: pl.BlockSpec((lead, rows, w), lambda b, c: (b, c, 0))
    seq_spec = lambda shape: pl.BlockSpec((lead,) + tuple(shape), lambda b, c: (b,) + (0,) * len(shape))
    if layer is None:
        state_spec = seq_spec(state_shape)
    else:
        state_spec = pl.BlockSpec((None, lead) + tuple(state_shape),
                                  lambda b, c: (layer, b) + (0,) * len(state_shape))
    param_spec = lambda a: pl.BlockSpec(a.shape, lambda b, c: (0,) * a.ndim)

    has_conv = hist is not None
    operands = list(acts) + ([hist] if has_conv else []) + [s0] + ([s_acc] if aliased else [])
    in_specs = ([act_spec(a.shape[-1]) for a in acts]
                + ([seq_spec(hist.shape[1:])] if has_conv else []) + [state_spec]
                + ([pl.BlockSpec(memory_space=pl.ANY)] if aliased else []))
    alias = {len(operands) - 1: 1} if aliased else {}
    operands += list(params)
    in_specs += [param_spec(p) for p in params]
    conv_w = acts[conv_idx].shape[-1] if has_conv else None
    xp_shape = (SUBLANES + rows, conv_w) if nb is None else (m, SUBLANES + rows, conv_w)
    s_out_shape = s_acc.shape if aliased else s0.shape
    kern = _scan_kernel(functools.partial(core, n=n, n_valid=n_valid), len(acts), conv_idx, has_conv,
                        len(params), state_shape[0], n, m, nb, aliased)
    return pl.pallas_call(
        kern,
        grid=grid,
        in_specs=in_specs,
        out_specs=[act_spec(out_w), state_spec],
        out_shape=[jax.ShapeDtypeStruct((bsz, t, out_w), F32), jax.ShapeDtypeStruct(s_out_shape, F32)],
        scratch_shapes=[pltpu.VMEM(xp_shape, F32)] if has_conv else [],
        input_output_aliases=alias,
        compiler_params=_cparams(2),
        name=f"{name}_{n}",
    )(*operands)


def _lane_row(pairs):
    row = jnp.zeros((SMALL_W,), F32)
    for off, vec in pairs:
        row = row.at[off:off + vec.shape[0]].set(vec.astype(F32))
    return row[None, :]


def _spread_matrix(blocks):
    total = sum(w for _, w in blocks)
    sel = np.zeros((SMALL_W, total), np.float32)
    col = 0
    for src, w in blocks:
        sel[src, col:col + w] = 1.0
        col += w
    return jnp.asarray(np.tile(sel, (3, 1)), BF16)


def _prep_matrices(params):
    sizes = (GDN_CONV, GDN_V, GDN_HEADS, GDN_HEADS, GLA_QK, GLA_QK, GLA_V, GLA_V, GLA_RANK,
             SSD_INNER, SSD_CONV, SSD_HEADS, N_GATES)
    offs = [0]
    for s in sizes:
        offs.append(offs[-1] + s)
    w_all = params["w_in"]
    (gdn_qkv, gdn_z, gdn_a, gdn_b, gla_q, gla_k, gla_v, gla_g, gla_r, ssd_z, ssd_xbc, ssd_dt,
     gates) = (w_all[..., offs[i]:offs[i + 1]] for i in range(len(sizes)))
    small = jnp.concatenate(
        [gdn_a, gdn_b, gla_r, ssd_dt,
         jnp.zeros(w_all.shape[:2] + (SMALL_W - 2 * GDN_HEADS - GLA_RANK - SSD_HEADS,), F32)], axis=-1)
    cols = dict(gdn_qkv=gdn_qkv, gdn_z=gdn_z, gla_qk=jnp.concatenate([gla_q, gla_k], axis=-1),
                gla_v=gla_v, gla_g=gla_g, ssd_z=ssd_z, ssd_xbc=ssd_xbc, small=small)
    mats = dict(w_in=jnp.concatenate([cols[name] for name, _ in PROJ_GROUPS], axis=-1).astype(BF16),
                w_gate=gates.astype(BF16))
    for name in ("ffn1_w_gate", "ffn1_w_up", "ffn1_w_down", "ffn2_w_gate", "ffn2_w_up", "ffn2_w_down",
                 "w_br_gdn", "w_br_gla", "w_br_ssd", "w_out"):
        mats[name] = params[name].astype(BF16)
    return mats


def _prep_layer(p, mats):
    wup = jnp.zeros((SMALL_W, GLA_QK), F32).at[SM_GLA_R:SM_GLA_R + GLA_RANK].set(p["gla_gate_up"])
    row = lambda v: v.astype(F32)[None, :]
    pad_rows = lambda rws: jnp.concatenate(rws + [jnp.zeros((SUBLANES - len(rws), SMALL_W), F32)], 0)
    ffn = lambda i: (row(p[f"ffn{i}_norm_pre"]), mats[f"ffn{i}_w_gate"], mats[f"ffn{i}_w_up"],
                     mats[f"ffn{i}_w_down"], row(p[f"ffn{i}_norm_post"]))
    return dict(
        ffn1=ffn(1), ffn2=ffn(2),
        mix_norm_pre=row(p["mix_norm_pre"]), mix_norm_post=row(p["mix_norm_post"]), w_in=mats["w_in"],
        w_gate=mats["w_gate"], b_merge=p["b_merge"].astype(F32),
        gdn_cw=p["gdn_conv_w"].T.astype(F32),
        gdn_hp=pad_rows([_lane_row([(SM_GDN_A, p["gdn_A_log"])]),
                         _lane_row([(SM_GDN_A, p["gdn_dt_bias"])])]),
        gdn_norm=row(p["gdn_norm"]),
        gdn_sel=_spread_matrix([(SM_GDN_A + h, GDN_DV) for h in range(GDN_HEADS)]
                               + [(SM_GDN_B + h, GDN_DV) for h in range(GDN_HEADS)]),
        gla_wup=wup.astype(BF16), gla_gb=row(p["gla_gate_bias"]), gla_norm=row(p["gla_norm"]),
        ssd_cw=p["ssd_conv_w"].T.astype(F32), ssd_cb=row(p["ssd_conv_b"]),
        ssd_hp=pad_rows([_lane_row([(SM_SSD_DT, p["ssd_A_log"])]),
                         _lane_row([(SM_SSD_DT, p["ssd_dt_bias"])])]),
        ssd_norm=row(p["ssd_norm"]),
        ssd_sel_p=_spread_matrix([(SM_SSD_DT + h, SSD_P) for h in range(SSD_HEADS)]),
        ssd_sel_n=_spread_matrix([(SM_SSD_DT + h, SSD_N) for h in range(SSD_HEADS)]),
        ssd_d=row(jnp.repeat(p["ssd_D"], SSD_P)),
        w_br_gdn=mats["w_br_gdn"], w_br_gla=mats["w_br_gla"], w_br_ssd=mats["w_br_ssd"],
        w_out=mats["w_out"])


def _layer(x, state, w, wl, bsz, t, n, n_valid, layer=None, s_acc=(None, None, None)):
    gdn_hist, s_gdn, s_gla, ssd_hist, s_ssd = state
    x = _ffn(x, *w["ffn1"], wl)
    fuse_conv = False
    conv = (t, gdn_hist, ssd_hist, w["gdn_cw"], w["ssd_cw"], w["ssd_cb"]) if fuse_conv else None
    outs = _inproj(x, w["mix_norm_pre"], w["w_in"], wl, conv)
    proj = dict(zip((name for name, _ in PROJ_GROUPS), outs))
    seq = lambda a: a.reshape(bsz, t, a.shape[-1])
    sm = seq(proj["small"])
    t_valid = t - (n - n_valid)
    if fuse_conv:
        gdn_hist = ssd_hist = None
        tails = [a[:, SUBLANES - (CONV_WIDTH - 1):] for a in outs[len(PROJ_GROUPS):]]
    else:
        tails = [seq(proj[g])[:, t_valid - (CONV_WIDTH - 1):t_valid] for g in CONV_GROUPS]
    o_gdn, s_gdn_new = _scan_call(
        "gdn", _gdn_core, [seq(proj["gdn_qkv"]), seq(proj["gdn_z"]), sm], 0, gdn_hist, s_gdn,
        [w["gdn_cw"], w["gdn_hp"], w["gdn_norm"], w["gdn_sel"]], GDN_V, n, n_valid, layer, s_acc[0])
    o_gla, s_gla_new = _scan_call(
        "gla", _gla_core, [seq(proj["gla_qk"]), seq(proj["gla_v"]), seq(proj["gla_g"]), sm], None,
        None, s_gla, [w["gla_wup"], w["gla_gb"], w["gla_norm"]], GLA_V, n, n_valid, layer, s_acc[1])
    o_ssd, s_ssd_new = _scan_call(
        "ssd", _ssd_core, [seq(proj["ssd_z"]), seq(proj["ssd_xbc"]), sm], 1, ssd_hist, s_ssd,
        [w["ssd_cw"], w["ssd_cb"], w["ssd_hp"], w["ssd_norm"], w["ssd_sel_p"], w["ssd_sel_n"],
         w["ssd_d"]], SSD_INNER, n, n_valid, layer, s_acc[2])
    flat = lambda a: a.reshape(bsz * t, a.shape[-1])
    x = _merge(x, flat(o_gdn), flat(o_gla), flat(o_ssd), w["mix_norm_pre"], w["w_gate"], w["b_merge"],
               w["w_br_gdn"], w["w_br_gla"], w["w_br_ssd"], w["w_out"], w["mix_norm_post"], wl)
    x = _ffn(x, *w["ffn2"], wl)
    return x, (tails[0], s_gdn_new, s_gla_new, tails[1], s_ssd_new)


def kernel(x_prompt, x_sample, state_gdn_conv, state_gdn, state_gla, state_ssd_conv, state_ssd,
           ffn1_norm_pre, ffn1_norm_post, ffn1_w_gate, ffn1_w_up, ffn1_w_down,
           mix_norm_pre, mix_norm_post, w_in, b_merge,
           gdn_conv_w, gdn_A_log, gdn_dt_bias, gdn_norm,
           gla_gate_up, gla_gate_bias, gla_norm,
           ssd_conv_w, ssd_conv_b, ssd_A_log, ssd_dt_bias, ssd_D, ssd_norm,
           w_br_gdn, w_br_gla, w_br_ssd, w_out,
           ffn2_norm_pre, ffn2_norm_post, ffn2_w_gate, ffn2_w_up, ffn2_w_down):
    params = dict(
        ffn1_norm_pre=ffn1_norm_pre, ffn1_norm_post=ffn1_norm_post, ffn1_w_gate=ffn1_w_gate,
        ffn1_w_up=ffn1_w_up, ffn1_w_down=ffn1_w_down,
        mix_norm_pre=mix_norm_pre, mix_norm_post=mix_norm_post, w_in=w_in, b_merge=b_merge,
        gdn_conv_w=gdn_conv_w, gdn_A_log=gdn_A_log, gdn_dt_bias=gdn_dt_bias, gdn_norm=gdn_norm,
        gla_gate_up=gla_gate_up, gla_gate_bias=gla_gate_bias, gla_norm=gla_norm,
        ssd_conv_w=ssd_conv_w, ssd_conv_b=ssd_conv_b, ssd_A_log=ssd_A_log, ssd_dt_bias=ssd_dt_bias,
        ssd_D=ssd_D, ssd_norm=ssd_norm,
        w_br_gdn=w_br_gdn, w_br_gla=w_br_gla, w_br_ssd=w_br_ssd, w_out=w_out,
        ffn2_norm_pre=ffn2_norm_pre, ffn2_norm_post=ffn2_norm_post, ffn2_w_gate=ffn2_w_gate,
        ffn2_w_up=ffn2_w_up, ffn2_w_down=ffn2_w_down)
    bp, tp, _ = x_prompt.shape
    bs, ts, _ = x_sample.shape
    ts_pad = SUBLANES
    n_p = min(CHUNK, tp)
    depth = state_gdn.shape[0]

    prompt_init = (jnp.zeros((bp, SUBLANES, GDN_CONV), F32),
                   jnp.zeros((bp, GDN_HEADS, GDN_DK, GDN_DV), F32),
                   jnp.zeros((bp, GLA_HEADS, GLA_DK, GLA_DV), F32),
                   jnp.zeros((bp, SUBLANES, SSD_CONV), F32),
                   jnp.zeros((bp, SSD_HEADS, SSD_P, SSD_N), F32))
    hist_rows = lambda s: jnp.pad(s.astype(F32), ((0, 0), (SUBLANES - (CONV_WIDTH - 1), 0), (0, 0)))
    sample_states = tuple(s.astype(F32) for s in (state_gdn, state_gla, state_ssd))
    s_acc = tuple(jnp.zeros(s.shape, F32) for s in sample_states)

    y_p = x_prompt.reshape(bp * tp, D_MODEL)
    y_s = jnp.pad(x_sample, ((0, 0), (0, ts_pad - ts), (0, 0))).reshape(bs * ts_pad, D_MODEL)
    new_p, s_conv = [], []
    mats = _prep_matrices(params)
    small = {name: arr for name, arr in params.items() if name not in mats and name != "w_in"}
    for l in range(depth):
        w = _prep_layer({name: arr[l] for name, arr in small.items()}, mats)
        y_p, st_p = _layer(y_p, prompt_init, w, l, bp, tp, n_p, n_p)
        sample_state = (hist_rows(state_gdn_conv[l]), sample_states[0], sample_states[1],
                        hist_rows(state_ssd_conv[l]), sample_states[2])
        y_s, st_s = _layer(y_s, sample_state, w, l, bs, ts_pad, ts_pad, ts, layer=l, s_acc=s_acc)
        new_p.append(st_p)
        s_conv.append((st_s[0], st_s[3]))
        s_acc = (st_s[1], st_s[2], st_s[4])
    p_out = tuple(jnp.stack(a) for a in zip(*new_p))
    s_gdn_conv, s_ssd_conv = (jnp.stack(a) for a in zip(*s_conv))
    y_prompt = y_p.reshape(bp, tp, D_MODEL)
    y_sample = y_s.reshape(bs, ts_pad, D_MODEL)[:, :ts]
    return (y_prompt, y_sample) + p_out + (s_gdn_conv, s_acc[0], s_acc[1], s_ssd_conv, s_acc[2])
```

```python
import functools

import jax
import jax.numpy as jnp
import numpy as np
from jax import lax
from jax.experimental import pallas as pl
from jax.experimental.pallas import tpu as pltpu

F32 = jnp.float32
BF16 = jnp.bfloat16

D_MODEL = 1024
DEPTH = 2
D_FF = 2816
MXU_WIDTH = 256
NORM_EPS = 1e-6
CONV_WIDTH = 4
CHUNK = 64
SUBLANES = 8

GDN_HEADS, GDN_DK, GDN_DV = 4, 128, 128
GLA_HEADS, GLA_DK, GLA_DV = 4, 64, 128
GLA_RANK = 16
GLA_GATE_NORMALIZER = 16.0
GLA_SUB = 16
SSD_HEADS, SSD_P, SSD_N, SSD_GROUPS = 8, 64, 128, 2

GDN_QK = GDN_HEADS * GDN_DK
GDN_V = GDN_HEADS * GDN_DV
GDN_CONV = 2 * GDN_QK + GDN_V
GLA_QK = GLA_HEADS * GLA_DK
GLA_V = GLA_HEADS * GLA_DV
SSD_INNER = SSD_HEADS * SSD_P
SSD_BC = SSD_GROUPS * SSD_N
SSD_CONV = SSD_INNER + 2 * SSD_BC
N_GATES = 3 * D_MODEL
SMALL_W = 128
SM_GDN_A, SM_GDN_B, SM_GLA_R, SM_SSD_DT = 0, 4, 8, 24

PROJ_GROUPS = (("gdn_qkv", GDN_CONV), ("ssd_xbc", SSD_CONV), ("gdn_z", GDN_V), ("gla_qk", 2 * GLA_QK),
               ("gla_v", GLA_V), ("gla_g", GLA_V), ("ssd_z", SSD_INNER), ("small", SMALL_W))
D_PROJ = sum(w for _, w in PROJ_GROUPS)
CONV_GROUPS = ("gdn_qkv", "ssd_xbc")
GATE_COL = -(-D_PROJ // N_GATES) * N_GATES

VMEM_LIMIT = 56 * 1024 * 1024
ROW_TILE = 512
INPROJ_ROW_TILE = 512
PROMPT_CHUNKS_PER_STEP = dict(gdn=4, gla=8, ssd=4)
DECODE_SEQS_PER_STEP = 8
DECODE_SEQS_TOGETHER = 8


def _cparams(n_axes):
    return pltpu.CompilerParams(dimension_semantics=("arbitrary",) * n_axes,
                                vmem_limit_bytes=VMEM_LIMIT)


def _mm(a, b):
    return jnp.dot(a.astype(BF16), b.astype(BF16), preferred_element_type=F32)


def _mm_nt(a, b):
    return lax.dot_general(a.astype(BF16), b.astype(BF16), (((1,), (1,)), ((), ())),
                           preferred_element_type=F32)


def _mm_tn(a, b):
    return lax.dot_general(a, b, (((0,), (0,)), ((), ())), preferred_element_type=F32)


def _split2(x):
    hi = x.astype(BF16).astype(F32)
    return hi, x - hi


def _split3_lanes(x):
    hi = x.astype(BF16)
    r = x - hi.astype(F32)
    mid = r.astype(BF16)
    lo = (r - mid.astype(F32)).astype(BF16)
    return jnp.concatenate([hi, mid, lo], axis=1)


def _mm_x3(a2, b2):
    (ah, al), (bh, bl) = a2, b2
    lhs = jnp.concatenate([ah, al, ah], axis=1).astype(BF16)
    rhs = jnp.concatenate([bh, bh, bl], axis=0).astype(BF16)
    return jnp.dot(lhs, rhs, preferred_element_type=F32)


def _cumsum_rows(tri, x):
    w = x.shape[1]
    y = jnp.dot(tri, _split3_lanes(x), preferred_element_type=F32)
    return y[:, :w] + y[:, w:2 * w] + y[:, 2 * w:]


def _lane_spread(x, sel3_ref):
    return jnp.dot(_split3_lanes(x), sel3_ref[...], preferred_element_type=F32)


def _rms(x):
    return x * lax.rsqrt(jnp.mean(x * x, axis=-1, keepdims=True) + NORM_EPS)


def _l2n(x):
    return x * lax.rsqrt(jnp.sum(x * x, axis=-1, keepdims=True) + NORM_EPS)


def _silu(x):
    return x * jax.nn.sigmoid(x)


def _softplus(x):
    return jnp.maximum(x, 0.0) + jnp.log1p(jnp.exp(-jnp.abs(x)))


def _log_sigmoid(x):
    return jnp.minimum(x, 0.0) - jnp.log1p(jnp.exp(-jnp.abs(x)))


def _iota2(n, m, dim):
    return lax.broadcasted_iota(jnp.int32, (n, m), dim)


def _masked_decay(diff, mask):
    return jnp.where(mask, jnp.exp(jnp.where(mask, diff, 0.0)), 0.0)


def _rows_as_lanes(x, n_rows):
    if n_rows < 128:
        x = jnp.concatenate([x, jnp.zeros((128 - n_rows, 128), F32)], axis=0)
    return x.T


def _conv_taps(xp, cw_ref, n_rows):
    x = xp[0:SUBLANES + n_rows, :]
    z = x * cw_ref[0:1, :]
    for i in range(1, CONV_WIDTH):
        z = pltpu.roll(z, 1, axis=0) + x * cw_ref[i:i + 1, :]
    return z[SUBLANES:, :]


def _conv_rows(xp, x_ref, cw_ref, n_rows):
    xp[SUBLANES:SUBLANES + n_rows, :] = x_ref[...]
    return _conv_taps(xp, cw_ref, n_rows)


def _each(fn, *lists):
    return [fn(*args) for args in zip(*lists)]


def _unit_lower_solve(a, rhs, n, n_valid):
    if n == CHUNK:
        r, c = _iota2(n, n, 0), _iota2(n, n, 1)
        eye = (r == c).astype(F32)
        same_block = lax.shift_right_logical(r, 4) == lax.shift_right_logical(c, 4)
        sp = lambda xs: [_split2(x) for x in xs]
        times = lambda xs, ys: _each(_mm_x3, xs, ys)
        plus = lambda xs, ys: _each(lambda x, y: x + y, xs, ys)
        d = _each(lambda x: jnp.where(same_block, x, 0.0), a)
        off = _each(lambda x, y: x - y, a, d)
        d_s = sp(d)
        d2_s = sp(times(d_s, d_s))
        t = _each(lambda x: eye - x, d)
        t = plus(t, times(sp(t), d2_s))
        d4_s = sp(times(d2_s, d2_s))
        t = plus(t, times(sp(t), d4_s))
        d8_s = sp(times(d4_s, d4_s))
        t_s = sp(plus(t, times(sp(t), d8_s)))
        b_s = sp(times(t_s, sp(off)))
        y = times(t_s, sp(rhs))
        b2_s = sp(times(b_s, b_s))
        z = _each(lambda yy, by: yy - by, y, times(b_s, sp(y)))
        return plus(z, times(b2_s, sp(z)))
    sols = []
    for ah, rh in zip(a, rhs):
        rows = [rh[0:1]]
        for i in range(1, n_valid):
            x = rh[i:i + 1]
            for j in range(i):
                x = x - ah[i:i + 1, j:j + 1] * rows[j]
            rows.append(x)
        if n_valid < n:
            rows.append(rh[n_valid:n])
        sols.append(jnp.concatenate(rows, axis=0))
    return sols


def _ffn_kernel(x_ref, npre_ref, wg_ref, wu_ref, wd_ref, npost_ref, o_ref, h_ref, a_ref, f_ref):
    x = x_ref[...]
    h_ref[...] = (_rms(x) * npre_ref[...]).astype(BF16)
    for c in range(0, D_FF, MXU_WIDTH):
        cols = slice(c, c + MXU_WIDTH)
        g = jnp.dot(h_ref[...], wg_ref[:, cols], preferred_element_type=F32)
        u = jnp.dot(h_ref[...], wu_ref[:, cols], preferred_element_type=F32)
        a_ref[:, cols] = (_silu(g) * u).astype(BF16)
    for c in range(0, D_MODEL, MXU_WIDTH):
        cols = slice(c, c + MXU_WIDTH)
        f_ref[:, cols] = jnp.dot(a_ref[...], wd_ref[:, cols], preferred_element_type=F32)
    o_ref[...] = x + 0.5 * (_rms(f_ref[...]) * npost_ref[...])


def _of_layer(a, layer):
    return pl.BlockSpec((None,) + a.shape[1:], lambda *_: (layer, 0, 0))


def _ffn(x, npre, wg, wu, wd, npost, layer):
    n = x.shape[0]
    tm = min(ROW_TILE, n)
    const = lambda i: (0, 0)
    return pl.pallas_call(
        _ffn_kernel,
        grid=(n // tm,),
        in_specs=[pl.BlockSpec((tm, D_MODEL), lambda i: (i, 0)),
                  pl.BlockSpec((1, D_MODEL), const),
                  _of_layer(wg, layer), _of_layer(wu, layer), _of_layer(wd, layer),
                  pl.BlockSpec((1, D_MODEL), const)],
        out_specs=pl.BlockSpec((tm, D_MODEL), lambda i: (i, 0)),
        out_shape=jax.ShapeDtypeStruct((n, D_MODEL), F32),
        scratch_shapes=[pltpu.VMEM((tm, D_MODEL), BF16), pltpu.VMEM((tm, D_FF), BF16),
                        pltpu.VMEM((tm, D_MODEL), F32)],
        compiler_params=_cparams(1),
        name=f"ffn_{n}",
    )(x, npre, wg, wu, wd, npost)


def _inproj_kernel(x_ref, n_ref, w_ref, *refs):
    o_refs, h_ref = refs[:-1], refs[-1]
    h_ref[...] = (_rms(x_ref[...]) * n_ref[...]).astype(BF16)
    col = 0
    for (_, width), o_ref in zip(PROJ_GROUPS, o_refs):
        step = min(width, 512)
        for s in range(0, width, step):
            o_ref[:, s:s + step] = jnp.dot(h_ref[...], w_ref[:, col + s:col + s + step],
                                           preferred_element_type=F32)
        col += width


def _inproj(x, norm, w, layer):
    n = x.shape[0]
    tm = min(INPROJ_ROW_TILE, n)
    return pl.pallas_call(
        _inproj_kernel,
        grid=(n // tm,),
        in_specs=[pl.BlockSpec((tm, D_MODEL), lambda i: (i, 0)),
                  pl.BlockSpec((1, D_MODEL), lambda i: (0, 0)),
                  pl.BlockSpec((None, D_MODEL, D_PROJ), lambda i: (layer, 0, 0))],
        out_specs=[pl.BlockSpec((tm, wd), lambda i: (i, 0)) for _, wd in PROJ_GROUPS],
        out_shape=[jax.ShapeDtypeStruct((n, wd), F32) for _, wd in PROJ_GROUPS],
        scratch_shapes=[pltpu.VMEM((tm, D_MODEL), BF16)],
        compiler_params=_cparams(1),
        name=f"inproj_{n}",
    )(x, norm, w)


def _merge_kernel(x_ref, og_ref, ol_ref, os_ref, npre_ref, wgate_ref, bm_ref, wg_ref, wl_ref, ws_ref,
                  wo_ref, npost_ref, o_ref):
    x = x_ref[...]
    h = (_rms(x) * npre_ref[...]).astype(BF16)
    merged = None
    for i, (b_ref, w_ref) in enumerate(((og_ref, wg_ref), (ol_ref, wl_ref), (os_ref, ws_ref))):
        logits = jnp.dot(h, wgate_ref[:, i * D_MODEL:(i + 1) * D_MODEL], preferred_element_type=F32)
        term = jax.nn.sigmoid(logits + bm_ref[i:i + 1, :]) * _mm(b_ref[...], w_ref[...])
        merged = term if merged is None else merged + term
    out = _mm(merged, wo_ref[...])
    o_ref[...] = x + _rms(out) * npost_ref[...]


def _merge(x, o_gdn, o_gla, o_ssd, npre, w_proj, b_merge, w_gdn, w_gla, w_ssd, w_out, npost, layer):
    n = x.shape[0]
    tm = min(ROW_TILE, n)
    row = lambda w: pl.BlockSpec((tm, w), lambda i: (i, 0))
    full = lambda a: _of_layer(a, layer) if a.ndim == 3 else pl.BlockSpec(a.shape, lambda i: (0, 0))
    gate_spec = pl.BlockSpec((None, D_MODEL, N_GATES), lambda i: (layer, 0, GATE_COL // N_GATES))
    consts = (npre, w_proj, b_merge, w_gdn, w_gla, w_ssd, w_out, npost)
    return pl.pallas_call(
        _merge_kernel,
        grid=(n // tm,),
        in_specs=[row(D_MODEL), row(GDN_V), row(GLA_V), row(SSD_INNER)]
        + [gate_spec if a is w_proj else full(a) for a in consts],
        out_specs=row(D_MODEL),
        out_shape=jax.ShapeDtypeStruct((n, D_MODEL), F32),
        compiler_params=_cparams(1),
        name=f"merge_{n}",
    )(x, o_gdn, o_gla, o_ssd, *consts)


def _gdn_core(acts, conv, s_in, params, o_views, *, n, n_valid, chained):
    _, z_views, sm_views = acts
    _, hp_ref, nrm_ref, sel3_ref = params
    m, nh = len(conv), GDN_HEADS
    valid = _iota2(n, 1, 0) < n_valid
    pad = (lambda t: jnp.where(valid, t, 0.0)) if n_valid < n else (lambda t: t)
    r, c = _iota2(n, n, 0), _iota2(n, n, 1)
    causal, strict = r >= c, r > c
    lane = _iota2(n, SMALL_W, 1)

    qkv = [_silu(y) for y in conv]
    sm = [v[...] for v in sm_views]
    g_all = [pad(-jnp.exp(hp_ref[0:1, :]) * _softplus(x + hp_ref[1:2, :])) for x in sm]
    tri = causal.astype(F32).astype(BF16)
    gcum = [_cumsum_rows(tri, g) for g in g_all]
    gcum_t = [_rows_as_lanes(g, n) for g in gcum]
    units = [(j, h) for j in range(m) for h in range(nh)]
    beta_all = [pad(jax.nn.sigmoid(x)) for x in sm]
    if n == CHUNK:
        spread = [_lane_spread(jnp.where(lane < SM_GDN_B, g, b), sel3_ref)
                  for g, b in zip(gcum, beta_all)]
        gc = [spread[j][:, h * GDN_DV:(h + 1) * GDN_DV] for j, h in units]
        beta = [spread[j][:, GDN_V + h * GDN_DV:GDN_V + (h + 1) * GDN_DV] for j, h in units]
    else:
        wide = lambda col: jnp.broadcast_to(col, (n, GDN_DV))
        gc = [wide(gcum[j][:, SM_GDN_A + h:SM_GDN_A + h + 1]) for j, h in units]
        beta = [wide(beta_all[j][:, SM_GDN_B + h:SM_GDN_B + h + 1]) for j, h in units]
    gr = [gcum_t[j][SM_GDN_A + h:SM_GDN_A + h + 1, 0:n] for j, h in units]
    q = [pad(_l2n(qkv[j][:, h * GDN_DK:(h + 1) * GDN_DK]) * (GDN_DK ** -0.5)) for j, h in units]
    k = [pad(_l2n(qkv[j][:, GDN_QK + h * GDN_DK:GDN_QK + (h + 1) * GDN_DK])) for j, h in units]
    v = [pad(qkv[j][:, 2 * GDN_QK + h * GDN_DV:2 * GDN_QK + (h + 1) * GDN_DV]) for j, h in units]
    dec = _each(lambda x, y: _masked_decay(x[:, :n] - y, causal), gc, gr)
    eg = _each(jnp.exp, gc)
    kk = _each(_mm_nt, k, k)
    qk = _each(_mm_nt, q, k)
    a = _each(lambda b, x, d: jnp.where(strict, b[:, :n] * x * d, 0.0), beta, kk, dec)
    rhs = _each(lambda b, e, vv, kx: jnp.concatenate([b * vv, (b * e) * kx], axis=-1), beta, eg, v, k)
    sol = _unit_lower_solve(a, rhs, n, n_valid)
    att = _each(lambda x, d: x * d, qk, dec)
    q_in = _each(lambda x, e: x * e, q, eg)
    g_last = [x[n - 1:n, :] for x in gc]
    k_out = _each(lambda kx, gl, x: kx * jnp.exp(gl - x), k, g_last, gc)
    e_last = _each(jnp.exp, g_last)

    def advance(idx, s):
        u = [sol[i][:, :GDN_DV] - _mm(sol[i][:, GDN_DV:], ss) for i, ss in zip(idx, s)]
        o = [_mm(q_in[i], ss) + _mm(att[i], uu) for i, ss, uu in zip(idx, s, u)]
        s_new = [e_last[i] * ss + _mm_tn(k_out[i], uu) for i, ss, uu in zip(idx, s, u)]
        for i, oo in zip(idx, o):
            j, h = units[i]
            cols = slice(h * GDN_DV, (h + 1) * GDN_DV)
            o_views[j][:, cols] = (_rms(oo) * nrm_ref[...]) * _silu(z_views[j][:, cols])
        return s_new

    if chained:
        s = s_in
        for j in range(m):
            s = advance(list(range(j * nh, (j + 1) * nh)), s)
        return s
    s_new = advance(list(range(m * nh)), [s_in[j][h] for j, h in units])
    return [s_new[j * nh:(j + 1) * nh] for j in range(m)]


def _gla_core(acts, conv, s_in, params, o_views, *, n, n_valid, chained):
    qk_views, v_views, g_views, sm_views = acts
    wup_ref, gb_ref, nrm_ref = params
    m, nh = len(qk_views), GLA_HEADS
    rows = _iota2(n, 1, 0)
    valid = rows < n_valid
    pad = (lambda t: jnp.where(valid, t, 0.0)) if n_valid < n else (lambda t: t)
    r, c = _iota2(n, n, 0), _iota2(n, n, 1)
    causal = r >= c
    segs = list(range(m))

    gk = [pad(_log_sigmoid(_mm(sm_views[j][...], wup_ref[...]) + gb_ref[...])
              * (1.0 / GLA_GATE_NORMALIZER)) for j in segs]
    q = [pad(qk_views[j][:, :GLA_QK] * (GLA_DK ** -0.5)) for j in segs]
    k = [pad(qk_views[j][:, GLA_QK:]) for j in segs]
    v = [pad(v_views[j][...]) for j in segs]
    tri = causal.astype(F32).astype(BF16)
    bc = [_cumsum_rows(tri, x) for x in gk]

    units = [(j, h) for j in segs for h in range(nh)]
    hs = [slice(h * GLA_DK, (h + 1) * GLA_DK) for h in range(nh)]
    vs = [slice(h * GLA_DV, (h + 1) * GLA_DV) for h in range(nh)]
    sub = min(GLA_SUB, n)
    att_rows = [[] for _ in units]
    for blk in range(n // sub):
        lo, hi = blk * sub, (blk + 1) * sub
        seen = rows < hi
        for j in segs:
            ref_row = bc[j][lo - 1:lo, :] if blk > 0 else jnp.zeros((1, GLA_QK), F32)
            qa = q[j][lo:hi] * jnp.exp(bc[j][lo:hi] - ref_row)
            ka = jnp.where(seen, k[j] * jnp.exp(jnp.where(seen, ref_row - bc[j], 0.0)), 0.0)
            for h in range(nh):
                att_rows[j * nh + h].append(_mm_nt(qa[:, hs[h]], ka[:, hs[h]]))
    att = [jnp.where(causal, x[0] if len(x) == 1 else jnp.concatenate(x, axis=0), 0.0)
           for x in att_rows]

    b_last = [x[n - 1:n, :] for x in bc]
    q_in = _each(lambda x, b: x * jnp.exp(b), q, bc)
    k_out = _each(lambda x, bl, b: x * jnp.exp(bl - b), k, b_last, bc)
    e_last = _each(jnp.exp, b_last)
    dpair = [[jnp.broadcast_to(e[:, p * 128:(p + 1) * 128], (128, 128)).T for p in range(nh // 2)]
             for e in e_last]
    dcol = [dpair[j][h // 2][(h % 2) * GLA_DK:(h % 2 + 1) * GLA_DK, :] for j, h in units]
    o_att = [_mm(att[i], v[j][:, vs[h]]) for i, (j, h) in enumerate(units)]
    upd = [_mm_tn(k_out[j][:, hs[h]], v[j][:, vs[h]]) for j, h in units]

    def emit(idx, s):
        o = [_mm(q_in[units[i][0]][:, hs[units[i][1]]], ss) + o_att[i] for i, ss in zip(idx, s)]
        for i, oo in zip(idx, o):
            j, h = units[i]
            o_views[j][:, vs[h]] = (_rms(oo) * nrm_ref[...]) * _silu(g_views[j][:, vs[h]])

    step = lambda idx, s: [dcol[i] * ss + upd[i] for i, ss in zip(idx, s)]
    everything = list(range(m * nh))
    if chained:
        starts, s = [], s_in
        for j in segs:
            starts += s
            s = step(list(range(j * nh, (j + 1) * nh)), s)
        emit(everything, starts)
        return s
    starts = [s_in[j][h] for j, h in units]
    emit(everything, starts)
    s_new = step(everything, starts)
    return [s_new[j * nh:(j + 1) * nh] for j in segs]


def _ssd_core(acts, conv, s_in, params, o_views, *, n, n_valid, chained):
    z_views, _, sm_views = acts
    _, cb_ref, hp_ref, nrm_ref, sel_p3_ref, sel_n3_ref, d_ref = params
    m, nh = len(conv), SSD_HEADS
    heads_per_group = nh // SSD_GROUPS
    group_w = SSD_INNER // SSD_GROUPS
    valid = _iota2(n, 1, 0) < n_valid
    pad = (lambda t: jnp.where(valid, t, 0.0)) if n_valid < n else (lambda t: t)
    r, c = _iota2(n, n, 0), _iota2(n, n, 1)
    causal = r >= c
    segs = list(range(m))

    xbc = [_silu(y + cb_ref[...]) for y in conv]
    dt_all = [pad(_softplus(sm_views[j][...] + hp_ref[1:2, :])) for j in segs]
    tri = causal.astype(F32).astype(BF16)
    acum = [_cumsum_rows(tri, -jnp.exp(hp_ref[0:1, :]) * x) for x in dt_all]
    acum_t = [_rows_as_lanes(x, n) for x in acum]
    dt_p = [_lane_spread(x, sel_p3_ref) for x in dt_all]
    ac_p = [_lane_spread(x, sel_p3_ref) for x in acum]
    ac_n = [_lane_spread(x, sel_n3_ref) for x in acum]
    xs = [x[:, :SSD_INNER] for x in xbc]
    xdt_all = _each(lambda x, d: x * d, xs, dt_p)
    x_out_all = _each(lambda x, a: x * jnp.exp(a[n - 1:n, :] - a), xdt_all, ac_p)
    skip_all = [d_ref[...] * x for x in xs]
    e_n = _each(jnp.exp, ac_n)
    pairs = [(j, g) for j in segs for g in range(SSD_GROUPS)]
    bm = [pad(xbc[j][:, SSD_INNER + g * SSD_N:SSD_INNER + (g + 1) * SSD_N]) for j, g in pairs]
    cm = [xbc[j][:, SSD_INNER + SSD_BC + g * SSD_N:SSD_INNER + SSD_BC + (g + 1) * SSD_N]
          for j, g in pairs]
    cb = _each(_mm_nt, cm, bm)

    units = [(j, h) for j in segs for h in range(nh)]
    grp = [j * SSD_GROUPS + h // heads_per_group for j, h in units]
    pcols = [slice(h * SSD_P, (h + 1) * SSD_P) for _, h in units]
    ncols = [slice(h * SSD_N, (h + 1) * SSD_N) for _, h in units]
    ar = [acum_t[j][SM_SSD_DT + h:SM_SSD_DT + h + 1, 0:n] for j, h in units]
    dec = [_masked_decay(ac_p[j][:, cs][:, :n] - rr, causal)
           for (j, _), cs, rr in zip(units, pcols, ar)]
    y_att = [_mm(cb[g] * d, xdt_all[j][:, cs]) for g, d, (j, _), cs in zip(grp, dec, units, pcols)]
    c_in = [cm[g] * e_n[j][:, cs] for g, (j, _), cs in zip(grp, units, ncols)]
    upd = [_mm_tn(x_out_all[j][:, cs], bm[g]) for g, (j, _), cs in zip(grp, units, pcols)]
    e_last = [e_n[j][n - 1:n, cs] for (j, _), cs in zip(units, ncols)]
    skip = [skip_all[j][:, cs] for (j, _), cs in zip(units, pcols)]

    def emit(idx, s):
        y = [y_att[i] + _mm_nt(c_in[i], ss) + skip[i] for i, ss in zip(idx, s)]
        y = [yy * _silu(z_views[units[i][0]][:, units[i][1] * SSD_P:(units[i][1] + 1) * SSD_P])
             for i, yy in zip(idx, y)]
        for p in sorted({grp[i] for i in idx}):
            mine = [(i, yy) for i, yy in zip(idx, y) if grp[i] == p]
            ms = sum(jnp.sum(yy * yy, axis=-1, keepdims=True) for _, yy in mine) * (1.0 / group_w)
            scale = lax.rsqrt(ms + NORM_EPS)
            for i, yy in mine:
                j, h = units[i]
                cols = slice(h * SSD_P, (h + 1) * SSD_P)
                o_views[j][:, cols] = (yy * scale) * nrm_ref[:, cols]

    step = lambda idx, s: [e_last[i] * ss + upd[i] for i, ss in zip(idx, s)]
    everything = list(range(m * nh))
    if chained:
        starts, s = [], s_in
        for j in segs:
            starts += s
            s = step(list(range(j * nh, (j + 1) * nh)), s)
        emit(everything, starts)
        return s
    starts = [s_in[j][h] for j, h in units]
    emit(everything, starts)
    s_new = step(everything, starts)
    return [s_new[j * nh:(j + 1) * nh] for j in segs]


def _scan_kernel(core, n_act, conv_idx, n_param, n_heads, n, m, nb, aliased):
    has_conv = conv_idx is not None
    chained = nb is None

    def kern(*refs):
        it = iter(refs)
        acts = [next(it) for _ in range(n_act)]
        hist = next(it) if has_conv else None
        s0 = next(it)
        if aliased:
            next(it)
        params = [next(it) for _ in range(n_param)]
        o_ref, s_ref = next(it), next(it)
        xp_ref = next(it) if has_conv else None

        if chained:
            @pl.when(pl.program_id(1) == 0)
            def _():
                s_ref[...] = s0[...]
                if has_conv:
                    xp_ref[0:SUBLANES, :] = hist[...]

            rows = lambda ref: [ref.at[pl.ds(j * n, n)] for j in range(m)]
            conv = None
            if has_conv:
                y = _conv_rows(xp_ref, acts[conv_idx], params[0], m * n)
                xp_ref[0:SUBLANES, :] = xp_ref[m * n:m * n + SUBLANES, :]
                conv = [y[j * n:(j + 1) * n] for j in range(m)]
            s_out = core([rows(a) for a in acts], conv, [s_ref[h] for h in range(n_heads)], params,
                         rows(o_ref), chained=True)
            for h in range(n_heads):
                s_ref[h] = s_out[h]
            return

        def step(i, carry):
            pick = lambda ref: [ref.at[i * m + j] for j in range(m)]
            conv = None
            if has_conv:
                conv = []
                for j, (hv, xv) in enumerate(zip(pick(hist), pick(acts[conv_idx]))):
                    xp = xp_ref.at[j]
                    xp[0:SUBLANES, :] = hv[...]
                    conv.append(_conv_rows(xp, xv, params[0], n))
            s_in = [[sv[h] for h in range(n_heads)] for sv in pick(s0)]
            s_out = core([pick(a) for a in acts], conv, s_in, params, pick(o_ref), chained=False)
            for sv, so in zip(pick(s_ref), s_out):
                for h in range(n_heads):
                    sv[h] = so[h]
            return carry

        lax.fori_loop(0, nb // m, step, 0)

    return kern


def _scan_call(name, core, acts, conv_idx, hist, s0, params, out_w, n, n_valid, layer, s_acc):
    bsz, t, _ = acts[0].shape
    nc = t // n
    state_shape = s0.shape[-3:]
    aliased = s_acc is not None
    if nc > 1:
        m = PROMPT_CHUNKS_PER_STEP[name] if nc % PROMPT_CHUNKS_PER_STEP[name] == 0 else 1
        nb, lead, grid = None, None, (bsz, nc // m)
    else:
        nb = DECODE_SEQS_PER_STEP if bsz % DECODE_SEQS_PER_STEP == 0 else 1
        m = DECODE_SEQS_TOGETHER if nb % DECODE_SEQS_TOGETHER == 0 else 1
        lead, grid = nb, (bsz // nb, 1)
    rows = n * m if nb is None else n

    act_spec = lambda w: pl.BlockSpec((lead, rows, w), lambda b, c: (b, c, 0))
    seq_spec = lambda shape: pl.BlockSpec((lead,) + tuple(shape), lambda b, c: (b,) + (0,) * len(shape))
    if layer is None:
        state_spec = seq_spec(state_shape)
    else:
        state_spec = pl.BlockSpec((None, lead) + tuple(state_shape),
                                  lambda b, c: (layer, b) + (0,) * len(state_shape))
    param_spec = lambda a: pl.BlockSpec(a.shape, lambda b, c: (0,) * a.ndim)

    has_conv = hist is not None
    operands = list(acts) + ([hist] if has_conv else []) + [s0] + ([s_acc] if aliased else [])
    in_specs = ([act_spec(a.shape[-1]) for a in acts]
                + ([seq_spec(hist.shape[1:])] if has_conv else []) + [state_spec]
                + ([pl.BlockSpec(memory_space=pl.ANY)] if aliased else []))
    alias = {len(operands) - 1: 1} if aliased else {}
    operands += list(params)
    in_specs += [param_spec(p) for p in params]
    conv_w = acts[conv_idx].shape[-1] if has_conv else None
    xp_shape = (SUBLANES + rows, conv_w) if nb is None else (m, SUBLANES + rows, conv_w)
    s_out_shape = s_acc.shape if aliased else s0.shape
    kern = _scan_kernel(functools.partial(core, n=n, n_valid=n_valid), len(acts), conv_idx,
                        len(params), state_shape[0], n, m, nb, aliased)
    return pl.pallas_call(
        kern,
        grid=grid,
        in_specs=in_specs,
        out_specs=[act_spec(out_w), state_spec],
        out_shape=[jax.ShapeDtypeStruct((bsz, t, out_w), F32), jax.ShapeDtypeStruct(s_out_shape, F32)],
        scratch_shapes=[pltpu.VMEM(xp_shape, F32)] if has_conv else [],
        input_output_aliases=alias,
        compiler_params=_cparams(2),
        name=f"{name}_{n}",
    )(*operands)


def _lane_row(pairs):
    row = jnp.zeros((SMALL_W,), F32)
    for off, vec in pairs:
        row = row.at[off:off + vec.shape[0]].set(vec.astype(F32))
    return row[None, :]


def _spread_matrix(blocks):
    total = sum(w for _, w in blocks)
    sel = np.zeros((SMALL_W, total), np.float32)
    col = 0
    for src, w in blocks:
        sel[src, col:col + w] = 1.0
        col += w
    return jnp.asarray(np.tile(sel, (3, 1)), BF16)


def _prep_matrices(params):
    sizes = (GDN_CONV, GDN_V, GDN_HEADS, GDN_HEADS, GLA_QK, GLA_QK, GLA_V, GLA_V, GLA_RANK,
             SSD_INNER, SSD_CONV, SSD_HEADS, N_GATES)
    offs = [0]
    for s in sizes:
        offs.append(offs[-1] + s)
    w_all = params["w_in"]
    (gdn_qkv, gdn_z, gdn_a, gdn_b, gla_q, gla_k, gla_v, gla_g, gla_r, ssd_z, ssd_xbc, ssd_dt,
     gates) = (w_all[..., offs[i]:offs[i + 1]] for i in range(len(sizes)))
    small = jnp.concatenate(
        [gdn_a, gdn_b, gla_r, ssd_dt,
         jnp.zeros(w_all.shape[:2] + (SMALL_W - 2 * GDN_HEADS - GLA_RANK - SSD_HEADS,), F32)], axis=-1)
    cols = dict(gdn_qkv=gdn_qkv, gdn_z=gdn_z, gla_qk=jnp.concatenate([gla_q, gla_k], axis=-1),
                gla_v=gla_v, gla_g=gla_g, ssd_z=ssd_z, ssd_xbc=ssd_xbc, small=small)
    gap = jnp.zeros(w_all.shape[:2] + (GATE_COL - D_PROJ,), F32)
    mats = dict(w_in=jnp.concatenate([cols[name] for name, _ in PROJ_GROUPS] + [gap, gates],
                                     axis=-1).astype(BF16))
    for name in ("ffn1_w_gate", "ffn1_w_up", "ffn1_w_down", "ffn2_w_gate", "ffn2_w_up", "ffn2_w_down",
                 "w_br_gdn", "w_br_gla", "w_br_ssd", "w_out"):
        mats[name] = params[name].astype(BF16)
    return mats


def _prep_layer(p, mats):
    wup = jnp.zeros((SMALL_W, GLA_QK), F32).at[SM_GLA_R:SM_GLA_R + GLA_RANK].set(p["gla_gate_up"])
    row = lambda v: v.astype(F32)[None, :]
    pad_rows = lambda rws: jnp.concatenate(rws + [jnp.zeros((SUBLANES - len(rws), SMALL_W), F32)], 0)
    ffn = lambda i: (row(p[f"ffn{i}_norm_pre"]), mats[f"ffn{i}_w_gate"], mats[f"ffn{i}_w_up"],
                     mats[f"ffn{i}_w_down"], row(p[f"ffn{i}_norm_post"]))
    return dict(
        ffn1=ffn(1), ffn2=ffn(2),
        mix_norm_pre=row(p["mix_norm_pre"]), mix_norm_post=row(p["mix_norm_post"]), w_in=mats["w_in"],
        b_merge=p["b_merge"].astype(F32),
        gdn_cw=p["gdn_conv_w"].T.astype(F32),
        gdn_hp=pad_rows([_lane_row([(SM_GDN_A, p["gdn_A_log"])]),
                         _lane_row([(SM_GDN_A, p["gdn_dt_bias"])])]),
        gdn_norm=row(p["gdn_norm"]),
        gdn_sel=_spread_matrix([(SM_GDN_A + h, GDN_DV) for h in range(GDN_HEADS)]
                               + [(SM_GDN_B + h, GDN_DV) for h in range(GDN_HEADS)]),
        gla_wup=wup.astype(BF16), gla_gb=row(p["gla_gate_bias"]), gla_norm=row(p["gla_norm"]),
        ssd_cw=p["ssd_conv_w"].T.astype(F32), ssd_cb=row(p["ssd_conv_b"]),
        ssd_hp=pad_rows([_lane_row([(SM_SSD_DT, p["ssd_A_log"])]),
                         _lane_row([(SM_SSD_DT, p["ssd_dt_bias"])])]),
        ssd_norm=row(p["ssd_norm"]),
        ssd_sel_p=_spread_matrix([(SM_SSD_DT + h, SSD_P) for h in range(SSD_HEADS)]),
        ssd_sel_n=_spread_matrix([(SM_SSD_DT + h, SSD_N) for h in range(SSD_HEADS)]),
        ssd_d=row(jnp.repeat(p["ssd_D"], SSD_P)),
        w_br_gdn=mats["w_br_gdn"], w_br_gla=mats["w_br_gla"], w_br_ssd=mats["w_br_ssd"],
        w_out=mats["w_out"])


def _layer(x, state, w, wl, bsz, t, n, n_valid, layer=None, s_acc=(None, None, None)):
    gdn_hist, s_gdn, s_gla, ssd_hist, s_ssd = state
    x = _ffn(x, *w["ffn1"], wl)
    proj = dict(zip((name for name, _ in PROJ_GROUPS), _inproj(x, w["mix_norm_pre"], w["w_in"], wl)))
    seq = lambda a: a.reshape(bsz, t, a.shape[-1])
    sm = seq(proj["small"])
    t_valid = t - (n - n_valid)
    tails = [seq(proj[g])[:, t_valid - (CONV_WIDTH - 1):t_valid] for g in CONV_GROUPS]
    o_gdn, s_gdn_new = _scan_call(
        "gdn", _gdn_core, [seq(proj["gdn_qkv"]), seq(proj["gdn_z"]), sm], 0, gdn_hist, s_gdn,
        [w["gdn_cw"], w["gdn_hp"], w["gdn_norm"], w["gdn_sel"]], GDN_V, n, n_valid, layer, s_acc[0])
    o_gla, s_gla_new = _scan_call(
        "gla", _gla_core, [seq(proj["gla_qk"]), seq(proj["gla_v"]), seq(proj["gla_g"]), sm], None,
        None, s_gla, [w["gla_wup"], w["gla_gb"], w["gla_norm"]], GLA_V, n, n_valid, layer, s_acc[1])
    o_ssd, s_ssd_new = _scan_call(
        "ssd", _ssd_core, [seq(proj["ssd_z"]), seq(proj["ssd_xbc"]), sm], 1, ssd_hist, s_ssd,
        [w["ssd_cw"], w["ssd_cb"], w["ssd_hp"], w["ssd_norm"], w["ssd_sel_p"], w["ssd_sel_n"],
         w["ssd_d"]], SSD_INNER, n, n_valid, layer, s_acc[2])
    flat = lambda a: a.reshape(bsz * t, a.shape[-1])
    x = _merge(x, flat(o_gdn), flat(o_gla), flat(o_ssd), w["mix_norm_pre"], w["w_in"], w["b_merge"],
               w["w_br_gdn"], w["w_br_gla"], w["w_br_ssd"], w["w_out"], w["mix_norm_post"], wl)
    x = _ffn(x, *w["ffn2"], wl)
    return x, (tails[0], s_gdn_new, s_gla_new, tails[1], s_ssd_new)


def kernel(x_prompt, x_sample, state_gdn_conv, state_gdn, state_gla, state_ssd_conv, state_ssd,
           ffn1_norm_pre, ffn1_norm_post, ffn1_w_gate, ffn1_w_up, ffn1_w_down,
           mix_norm_pre, mix_norm_post, w_in, b_merge,
           gdn_conv_w, gdn_A_log, gdn_dt_bias, gdn_norm,
           gla_gate_up, gla_gate_bias, gla_norm,
           ssd_conv_w, ssd_conv_b, ssd_A_log, ssd_dt_bias, ssd_D, ssd_norm,
           w_br_gdn, w_br_gla, w_br_ssd, w_out,
           ffn2_norm_pre, ffn2_norm_post, ffn2_w_gate, ffn2_w_up, ffn2_w_down):
    params = dict(
        ffn1_norm_pre=ffn1_norm_pre, ffn1_norm_post=ffn1_norm_post, ffn1_w_gate=ffn1_w_gate,
        ffn1_w_up=ffn1_w_up, ffn1_w_down=ffn1_w_down,
        mix_norm_pre=mix_norm_pre, mix_norm_post=mix_norm_post, w_in=w_in, b_merge=b_merge,
        gdn_conv_w=gdn_conv_w, gdn_A_log=gdn_A_log, gdn_dt_bias=gdn_dt_bias, gdn_norm=gdn_norm,
        gla_gate_up=gla_gate_up, gla_gate_bias=gla_gate_bias, gla_norm=gla_norm,
        ssd_conv_w=ssd_conv_w, ssd_conv_b=ssd_conv_b, ssd_A_log=ssd_A_log, ssd_dt_bias=ssd_dt_bias,
        ssd_D=ssd_D, ssd_norm=ssd_norm,
        w_br_gdn=w_br_gdn, w_br_gla=w_br_gla, w_br_ssd=w_br_ssd, w_out=w_out,
        ffn2_norm_pre=ffn2_norm_pre, ffn2_norm_post=ffn2_norm_post, ffn2_w_gate=ffn2_w_gate,
        ffn2_w_up=ffn2_w_up, ffn2_w_down=ffn2_w_down)
    bp, tp, _ = x_prompt.shape
    bs, ts, _ = x_sample.shape
    ts_pad = SUBLANES
    n_p = min(CHUNK, tp)
    depth = state_gdn.shape[0]

    prompt_init = (jnp.zeros((bp, SUBLANES, GDN_CONV), F32),
                   jnp.zeros((bp, GDN_HEADS, GDN_DK, GDN_DV), F32),
                   jnp.zeros((bp, GLA_HEADS, GLA_DK, GLA_DV), F32),
                   jnp.zeros((bp, SUBLANES, SSD_CONV), F32),
                   jnp.zeros((bp, SSD_HEADS, SSD_P, SSD_N), F32))
    hist_rows = lambda s: jnp.pad(s.astype(F32), ((0, 0), (SUBLANES - (CONV_WIDTH - 1), 0), (0, 0)))
    sample_states = tuple(s.astype(F32) for s in (state_gdn, state_gla, state_ssd))
    s_acc = tuple(jnp.zeros(s.shape, F32) for s in sample_states)

    y_p = x_prompt.reshape(bp * tp, D_MODEL)
    y_s = jnp.pad(x_sample, ((0, 0), (0, ts_pad - ts), (0, 0))).reshape(bs * ts_pad, D_MODEL)
    new_p, s_conv = [], []
    mats = _prep_matrices(params)
    small = {name: arr for name, arr in params.items() if name not in mats and name != "w_in"}
    for l in range(depth):
        w = _prep_layer({name: arr[l] for name, arr in small.items()}, mats)
        y_p, st_p = _layer(y_p, prompt_init, w, l, bp, tp, n_p, n_p)
        sample_state = (hist_rows(state_gdn_conv[l]), sample_states[0], sample_states[1],
                        hist_rows(state_ssd_conv[l]), sample_states[2])
        y_s, st_s = _layer(y_s, sample_state, w, l, bs, ts_pad, ts_pad, ts, layer=l, s_acc=s_acc)
        new_p.append(st_p)
        s_conv.append((st_s[0], st_s[3]))
        s_acc = (st_s[1], st_s[2], st_s[4])
    p_out = tuple(jnp.stack(a) for a in zip(*new_p))
    s_gdn_conv, s_ssd_conv = (jnp.stack(a) for a in zip(*s_conv))
    y_prompt = y_p.reshape(bp, tp, D_MODEL)
    y_sample = y_s.reshape(bs, ts_pad, D_MODEL)[:, :ts]
    return (y_prompt, y_sample) + p_out + (s_gdn_conv, s_acc[0], s_acc[1], s_ssd_conv, s_acc[2])
```

```python
import functools

import jax
import jax.numpy as jnp
import numpy as np
from jax import lax
from jax.experimental import pallas as pl
from jax.experimental.pallas import tpu as pltpu

F32 = jnp.float32
BF16 = jnp.bfloat16

D_MODEL = 1024
DEPTH = 2
D_FF = 2816
MXU_WIDTH = 256
NORM_EPS = 1e-6
CONV_WIDTH = 4
CHUNK = 64
SUBLANES = 8

GDN_HEADS, GDN_DK, GDN_DV = 4, 128, 128
GLA_HEADS, GLA_DK, GLA_DV = 4, 64, 128
GLA_RANK = 16
GLA_GATE_NORMALIZER = 16.0
GLA_SUB = 16
SSD_HEADS, SSD_P, SSD_N, SSD_GROUPS = 8, 64, 128, 2

GDN_QK = GDN_HEADS * GDN_DK
GDN_V = GDN_HEADS * GDN_DV
GDN_CONV = 2 * GDN_QK + GDN_V
GLA_QK = GLA_HEADS * GLA_DK
GLA_V = GLA_HEADS * GLA_DV
SSD_INNER = SSD_HEADS * SSD_P
SSD_BC = SSD_GROUPS * SSD_N
SSD_CONV = SSD_INNER + 2 * SSD_BC
N_GATES = 3 * D_MODEL
SMALL_W = 128
SM_GDN_A, SM_GDN_B, SM_GLA_R, SM_SSD_DT = 0, 4, 8, 24

PROJ_GROUPS = (("gdn_qkv", GDN_CONV), ("ssd_xbc", SSD_CONV), ("gdn_z", GDN_V), ("gla_qk", 2 * GLA_QK),
               ("gla_v", GLA_V), ("gla_g", GLA_V), ("ssd_z", SSD_INNER), ("small", SMALL_W))
D_PROJ = sum(w for _, w in PROJ_GROUPS)
CONV_GROUPS = ("gdn_qkv", "ssd_xbc")
GATE_COL = -(-D_PROJ // N_GATES) * N_GATES

VMEM_LIMIT = 56 * 1024 * 1024
ROW_TILE = 512
INPROJ_ROW_TILE = 512
PROMPT_CHUNKS_PER_STEP = dict(gdn=4, gla=8, ssd=4)
DECODE_SEQS_PER_STEP = 8
DECODE_SEQS_TOGETHER = 8


def _cparams(n_axes):
    return pltpu.CompilerParams(dimension_semantics=("arbitrary",) * n_axes,
                                vmem_limit_bytes=VMEM_LIMIT)


def _mm(a, b):
    return jnp.dot(a.astype(BF16), b.astype(BF16), preferred_element_type=F32)


def _mm_nt(a, b):
    return lax.dot_general(a.astype(BF16), b.astype(BF16), (((1,), (1,)), ((), ())),
                           preferred_element_type=F32)


def _mm_tn(a, b):
    return lax.dot_general(a, b, (((0,), (0,)), ((), ())), preferred_element_type=F32)


def _split2(x):
    hi = x.astype(BF16).astype(F32)
    return hi, x - hi


def _split3_lanes(x):
    hi = x.astype(BF16)
    r = x - hi.astype(F32)
    mid = r.astype(BF16)
    lo = (r - mid.astype(F32)).astype(BF16)
    return jnp.concatenate([hi, mid, lo], axis=1)


def _mm_x3(a2, b2):
    (ah, al), (bh, bl) = a2, b2
    lhs = jnp.concatenate([ah, al, ah], axis=1).astype(BF16)
    rhs = jnp.concatenate([bh, bh, bl], axis=0).astype(BF16)
    return jnp.dot(lhs, rhs, preferred_element_type=F32)


def _cumsum_rows(tri, x):
    w = x.shape[1]
    y = jnp.dot(tri, _split3_lanes(x), preferred_element_type=F32)
    return y[:, :w] + y[:, w:2 * w] + y[:, 2 * w:]


def _lane_spread(x, sel3_ref):
    return jnp.dot(_split3_lanes(x), sel3_ref[...], preferred_element_type=F32)


def _rms(x):
    return x * lax.rsqrt(jnp.mean(x * x, axis=-1, keepdims=True) + NORM_EPS)


def _l2n(x):
    return x * lax.rsqrt(jnp.sum(x * x, axis=-1, keepdims=True) + NORM_EPS)


def _silu(x):
    return x * jax.nn.sigmoid(x)


def _softplus(x):
    return jnp.maximum(x, 0.0) + jnp.log1p(jnp.exp(-jnp.abs(x)))


def _log_sigmoid(x):
    return jnp.minimum(x, 0.0) - jnp.log1p(jnp.exp(-jnp.abs(x)))


def _iota2(n, m, dim):
    return lax.broadcasted_iota(jnp.int32, (n, m), dim)


def _masked_decay(diff, mask):
    return jnp.where(mask, jnp.exp(jnp.where(mask, diff, 0.0)), 0.0)


def _rows_as_lanes(x, n_rows):
    if n_rows < 128:
        x = jnp.concatenate([x, jnp.zeros((128 - n_rows, 128), F32)], axis=0)
    return x.T


def _conv_taps(xp, cw_ref, n_rows):
    x = xp[0:SUBLANES + n_rows, :]
    z = x * cw_ref[0:1, :]
    for i in range(1, CONV_WIDTH):
        z = pltpu.roll(z, 1, axis=0) + x * cw_ref[i:i + 1, :]
    return z[SUBLANES:, :]


def _conv_rows(xp, x_ref, cw_ref, n_rows):
    xp[SUBLANES:SUBLANES + n_rows, :] = x_ref[...]
    return _conv_taps(xp, cw_ref, n_rows)


def _each(fn, *lists):
    return [fn(*args) for args in zip(*lists)]


def _unit_lower_solve(a, rhs, n, n_valid):
    if n == CHUNK:
        r, c = _iota2(n, n, 0), _iota2(n, n, 1)
        eye = (r == c).astype(F32)
        same_block = lax.shift_right_logical(r, 4) == lax.shift_right_logical(c, 4)
        sp = lambda xs: [_split2(x) for x in xs]
        times = lambda xs, ys: _each(_mm_x3, xs, ys)
        plus = lambda xs, ys: _each(lambda x, y: x + y, xs, ys)
        d = _each(lambda x: jnp.where(same_block, x, 0.0), a)
        off = _each(lambda x, y: x - y, a, d)
        d_s = sp(d)
        d2_s = sp(times(d_s, d_s))
        t = _each(lambda x: eye - x, d)
        t = plus(t, times(sp(t), d2_s))
        d4_s = sp(times(d2_s, d2_s))
        t = plus(t, times(sp(t), d4_s))
        d8_s = sp(times(d4_s, d4_s))
        t_s = sp(plus(t, times(sp(t), d8_s)))
        b_s = sp(times(t_s, sp(off)))
        y = times(t_s, sp(rhs))
        b2_s = sp(times(b_s, b_s))
        z = _each(lambda yy, by: yy - by, y, times(b_s, sp(y)))
        return plus(z, times(b2_s, sp(z)))
    sols = []
    for ah, rh in zip(a, rhs):
        rows = [rh[0:1]]
        for i in range(1, n_valid):
            x = rh[i:i + 1]
            for j in range(i):
                x = x - ah[i:i + 1, j:j + 1] * rows[j]
            rows.append(x)
        if n_valid < n:
            rows.append(rh[n_valid:n])
        sols.append(jnp.concatenate(rows, axis=0))
    return sols


def _ffn_kernel(x_ref, npre_ref, wg_ref, wu_ref, wd_ref, npost_ref, o_ref, h_ref, a_ref, f_ref):
    x = x_ref[...]
    h_ref[...] = (_rms(x) * npre_ref[...]).astype(BF16)
    for c in range(0, D_FF, MXU_WIDTH):
        cols = slice(c, c + MXU_WIDTH)
        g = jnp.dot(h_ref[...], wg_ref[:, cols], preferred_element_type=F32)
        u = jnp.dot(h_ref[...], wu_ref[:, cols], preferred_element_type=F32)
        a_ref[:, cols] = (_silu(g) * u).astype(BF16)
    for c in range(0, D_MODEL, MXU_WIDTH):
        cols = slice(c, c + MXU_WIDTH)
        f_ref[:, cols] = jnp.dot(a_ref[...], wd_ref[:, cols], preferred_element_type=F32)
    o_ref[...] = x + 0.5 * (_rms(f_ref[...]) * npost_ref[...])


def _of_layer(a, layer):
    return pl.BlockSpec((None,) + a.shape[1:], lambda *_: (layer, 0, 0))


def _ffn(x, npre, wg, wu, wd, npost, layer):
    n = x.shape[0]
    tm = min(ROW_TILE, n)
    const = lambda i: (0, 0)
    return pl.pallas_call(
        _ffn_kernel,
        grid=(n // tm,),
        in_specs=[pl.BlockSpec((tm, D_MODEL), lambda i: (i, 0)),
                  pl.BlockSpec((1, D_MODEL), const),
                  _of_layer(wg, layer), _of_layer(wu, layer), _of_layer(wd, layer),
                  pl.BlockSpec((1, D_MODEL), const)],
        out_specs=pl.BlockSpec((tm, D_MODEL), lambda i: (i, 0)),
        out_shape=jax.ShapeDtypeStruct((n, D_MODEL), F32),
        scratch_shapes=[pltpu.VMEM((tm, D_MODEL), BF16), pltpu.VMEM((tm, D_FF), BF16),
                        pltpu.VMEM((tm, D_MODEL), F32)],
        compiler_params=_cparams(1),
        name=f"ffn_{n}",
    )(x, npre, wg, wu, wd, npost)


def _inproj_kernel(x_ref, n_ref, w_ref, *refs):
    o_refs, h_ref = refs[:-1], refs[-1]
    h_ref[...] = (_rms(x_ref[...]) * n_ref[...]).astype(BF16)
    col = 0
    for (_, width), o_ref in zip(PROJ_GROUPS, o_refs):
        step = min(width, 512)
        for s in range(0, width, step):
            o_ref[:, s:s + step] = jnp.dot(h_ref[...], w_ref[:, col + s:col + s + step],
                                           preferred_element_type=F32)
        col += width


def _inproj(x, norm, w, layer):
    n = x.shape[0]
    tm = min(INPROJ_ROW_TILE, n)
    return pl.pallas_call(
        _inproj_kernel,
        grid=(n // tm,),
        in_specs=[pl.BlockSpec((tm, D_MODEL), lambda i: (i, 0)),
                  pl.BlockSpec((1, D_MODEL), lambda i: (0, 0)),
                  pl.BlockSpec((None, D_MODEL, D_PROJ), lambda i: (layer, 0, 0))],
        out_specs=[pl.BlockSpec((tm, wd), lambda i: (i, 0)) for _, wd in PROJ_GROUPS],
        out_shape=[jax.ShapeDtypeStruct((n, wd), F32) for _, wd in PROJ_GROUPS],
        scratch_shapes=[pltpu.VMEM((tm, D_MODEL), BF16)],
        compiler_params=_cparams(1),
        name=f"inproj_{n}",
    )(x, norm, w)


def _merge_kernel(x_ref, og_ref, ol_ref, os_ref, npre_ref, wgate_ref, bm_ref, wg_ref, wl_ref, ws_ref,
                  wo_ref, npost_ref, o_ref):
    x = x_ref[...]
    h = (_rms(x) * npre_ref[...]).astype(BF16)
    merged = None
    for i, (b_ref, w_ref) in enumerate(((og_ref, wg_ref), (ol_ref, wl_ref), (os_ref, ws_ref))):
        logits = jnp.dot(h, wgate_ref[:, i * D_MODEL:(i + 1) * D_MODEL], preferred_element_type=F32)
        term = jax.nn.sigmoid(logits + bm_ref[i:i + 1, :]) * _mm(b_ref[...], w_ref[...])
        merged = term if merged is None else merged + term
    out = _mm(merged, wo_ref[...])
    o_ref[...] = x + _rms(out) * npost_ref[...]


def _merge(x, o_gdn, o_gla, o_ssd, npre, w_proj, b_merge, w_gdn, w_gla, w_ssd, w_out, npost, layer):
    n = x.shape[0]
    tm = min(ROW_TILE, n)
    row = lambda w: pl.BlockSpec((tm, w), lambda i: (i, 0))
    full = lambda a: _of_layer(a, layer) if a.ndim == 3 else pl.BlockSpec(a.shape, lambda i: (0, 0))
    gate_spec = pl.BlockSpec((None, D_MODEL, N_GATES), lambda i: (layer, 0, GATE_COL // N_GATES))
    consts = (npre, w_proj, b_merge, w_gdn, w_gla, w_ssd, w_out, npost)
    return pl.pallas_call(
        _merge_kernel,
        grid=(n // tm,),
        in_specs=[row(D_MODEL), row(GDN_V), row(GLA_V), row(SSD_INNER)]
        + [gate_spec if a is w_proj else full(a) for a in consts],
        out_specs=row(D_MODEL),
        out_shape=jax.ShapeDtypeStruct((n, D_MODEL), F32),
        compiler_params=_cparams(1),
        name=f"merge_{n}",
    )(x, o_gdn, o_gla, o_ssd, *consts)


def _gdn_core(acts, conv, s_in, params, o_views, *, n, n_valid, chained):
    _, z_views, sm_views = acts
    _, hp_ref, nrm_ref, sel3_ref = params
    m, nh = len(conv), GDN_HEADS
    valid = _iota2(n, 1, 0) < n_valid
    pad = (lambda t: jnp.where(valid, t, 0.0)) if n_valid < n else (lambda t: t)
    r, c = _iota2(n, n, 0), _iota2(n, n, 1)
    causal, strict = r >= c, r > c
    lane = _iota2(n, SMALL_W, 1)

    qkv = [_silu(y) for y in conv]
    sm = [v[...] for v in sm_views]
    g_all = [pad(-jnp.exp(hp_ref[0:1, :]) * _softplus(x + hp_ref[1:2, :])) for x in sm]
    tri = causal.astype(F32).astype(BF16)
    gcum = [_cumsum_rows(tri, g) for g in g_all]
    gcum_t = [_rows_as_lanes(g, n) for g in gcum]
    units = [(j, h) for j in range(m) for h in range(nh)]
    beta_all = [pad(jax.nn.sigmoid(x)) for x in sm]
    if n == CHUNK:
        spread = [_lane_spread(jnp.where(lane < SM_GDN_B, g, b), sel3_ref)
                  for g, b in zip(gcum, beta_all)]
        gc = [spread[j][:, h * GDN_DV:(h + 1) * GDN_DV] for j, h in units]
        beta = [spread[j][:, GDN_V + h * GDN_DV:GDN_V + (h + 1) * GDN_DV] for j, h in units]
    else:
        wide = lambda col: jnp.broadcast_to(col, (n, GDN_DV))
        gc = [wide(gcum[j][:, SM_GDN_A + h:SM_GDN_A + h + 1]) for j, h in units]
        beta = [wide(beta_all[j][:, SM_GDN_B + h:SM_GDN_B + h + 1]) for j, h in units]
    gr = [gcum_t[j][SM_GDN_A + h:SM_GDN_A + h + 1, 0:n] for j, h in units]
    q = [pad(_l2n(qkv[j][:, h * GDN_DK:(h + 1) * GDN_DK]) * (GDN_DK ** -0.5)) for j, h in units]
    k = [pad(_l2n(qkv[j][:, GDN_QK + h * GDN_DK:GDN_QK + (h + 1) * GDN_DK])) for j, h in units]
    v = [pad(qkv[j][:, 2 * GDN_QK + h * GDN_DV:2 * GDN_QK + (h + 1) * GDN_DV]) for j, h in units]
    dec = _each(lambda x, y: _masked_decay(x[:, :n] - y, causal), gc, gr)
    eg = _each(jnp.exp, gc)
    kk = _each(_mm_nt, k, k)
    qk = _each(_mm_nt, q, k)
    a = _each(lambda b, x, d: jnp.where(strict, b[:, :n] * x * d, 0.0), beta, kk, dec)
    rhs = _each(lambda b, e, vv, kx: jnp.concatenate([b * vv, (b * e) * kx], axis=-1), beta, eg, v, k)
    sol = _unit_lower_solve(a, rhs, n, n_valid)
    att = _each(lambda x, d: x * d, qk, dec)
    q_in = _each(lambda x, e: x * e, q, eg)
    g_last = [x[n - 1:n, :] for x in gc]
    k_out = _each(lambda kx, gl, x: kx * jnp.exp(gl - x), k, g_last, gc)
    e_last = _each(jnp.exp, g_last)

    def advance(idx, s):
        u = [sol[i][:, :GDN_DV] - _mm(sol[i][:, GDN_DV:], ss) for i, ss in zip(idx, s)]
        o = [_mm(q_in[i], ss) + _mm(att[i], uu) for i, ss, uu in zip(idx, s, u)]
        s_new = [e_last[i] * ss + _mm_tn(k_out[i], uu) for i, ss, uu in zip(idx, s, u)]
        for i, oo in zip(idx, o):
            j, h = units[i]
            cols = slice(h * GDN_DV, (h + 1) * GDN_DV)
            o_views[j][:, cols] = (_rms(oo) * nrm_ref[...]) * _silu(z_views[j][:, cols])
        return s_new

    if chained:
        s = s_in
        for j in range(m):
            s = advance(list(range(j * nh, (j + 1) * nh)), s)
        return s
    s_new = advance(list(range(m * nh)), [s_in[j][h] for j, h in units])
    return [s_new[j * nh:(j + 1) * nh] for j in range(m)]


def _gla_core(acts, conv, s_in, params, o_views, *, n, n_valid, chained):
    qk_views, v_views, g_views, sm_views = acts
    wup_ref, gb_ref, nrm_ref = params
    m, nh = len(qk_views), GLA_HEADS
    rows = _iota2(n, 1, 0)
    valid = rows < n_valid
    pad = (lambda t: jnp.where(valid, t, 0.0)) if n_valid < n else (lambda t: t)
    r, c = _iota2(n, n, 0), _iota2(n, n, 1)
    causal = r >= c
    segs = list(range(m))

    gk = [pad(_log_sigmoid(_mm(sm_views[j][...], wup_ref[...]) + gb_ref[...])
              * (1.0 / GLA_GATE_NORMALIZER)) for j in segs]
    q = [pad(qk_views[j][:, :GLA_QK] * (GLA_DK ** -0.5)) for j in segs]
    k = [pad(qk_views[j][:, GLA_QK:]) for j in segs]
    v = [pad(v_views[j][...]) for j in segs]
    tri = causal.astype(F32).astype(BF16)
    bc = [_cumsum_rows(tri, x) for x in gk]

    units = [(j, h) for j in segs for h in range(nh)]
    hs = [slice(h * GLA_DK, (h + 1) * GLA_DK) for h in range(nh)]
    vs = [slice(h * GLA_DV, (h + 1) * GLA_DV) for h in range(nh)]
    sub = min(GLA_SUB, n)
    att_rows = [[] for _ in units]
    for blk in range(n // sub):
        lo, hi = blk * sub, (blk + 1) * sub
        seen = rows < hi
        for j in segs:
            ref_row = bc[j][lo - 1:lo, :] if blk > 0 else jnp.zeros((1, GLA_QK), F32)
            qa = q[j][lo:hi] * jnp.exp(bc[j][lo:hi] - ref_row)
            ka = jnp.where(seen, k[j] * jnp.exp(jnp.where(seen, ref_row - bc[j], 0.0)), 0.0)
            for h in range(nh):
                att_rows[j * nh + h].append(_mm_nt(qa[:, hs[h]], ka[:, hs[h]]))
    att = [jnp.where(causal, x[0] if len(x) == 1 else jnp.concatenate(x, axis=0), 0.0)
           for x in att_rows]

    b_last = [x[n - 1:n, :] for x in bc]
    q_in = _each(lambda x, b: x * jnp.exp(b), q, bc)
    k_out = _each(lambda x, bl, b: x * jnp.exp(bl - b), k, b_last, bc)
    e_last = _each(jnp.exp, b_last)
    dpair = [[jnp.broadcast_to(e[:, p * 128:(p + 1) * 128], (128, 128)).T for p in range(nh // 2)]
             for e in e_last]
    dcol = [dpair[j][h // 2][(h % 2) * GLA_DK:(h % 2 + 1) * GLA_DK, :] for j, h in units]
    o_att = [_mm(att[i], v[j][:, vs[h]]) for i, (j, h) in enumerate(units)]
    upd = [_mm_tn(k_out[j][:, hs[h]], v[j][:, vs[h]]) for j, h in units]

    def emit(idx, s):
        o = [_mm(q_in[units[i][0]][:, hs[units[i][1]]], ss) + o_att[i] for i, ss in zip(idx, s)]
        for i, oo in zip(idx, o):
            j, h = units[i]
            o_views[j][:, vs[h]] = (_rms(oo) * nrm_ref[...]) * _silu(g_views[j][:, vs[h]])

    step = lambda idx, s: [dcol[i] * ss + upd[i] for i, ss in zip(idx, s)]
    everything = list(range(m * nh))
    if chained:
        starts, s = [], s_in
        for j in segs:
            starts += s
            s = step(list(range(j * nh, (j + 1) * nh)), s)
        emit(everything, starts)
        return s
    starts = [s_in[j][h] for j, h in units]
    emit(everything, starts)
    s_new = step(everything, starts)
    return [s_new[j * nh:(j + 1) * nh] for j in segs]


def _ssd_core(acts, conv, s_in, params, o_views, *, n, n_valid, chained):
    z_views, _, sm_views = acts
    _, cb_ref, hp_ref, nrm_ref, sel_p3_ref, sel_n3_ref, d_ref = params
    m, nh = len(conv), SSD_HEADS
    heads_per_group = nh // SSD_GROUPS
    group_w = SSD_INNER // SSD_GROUPS
    valid = _iota2(n, 1, 0) < n_valid
    pad = (lambda t: jnp.where(valid, t, 0.0)) if n_valid < n else (lambda t: t)
    r, c = _iota2(n, n, 0), _iota2(n, n, 1)
    causal = r >= c
    segs = list(range(m))

    xbc = [_silu(y + cb_ref[...]) for y in conv]
    dt_all = [pad(_softplus(sm_views[j][...] + hp_ref[1:2, :])) for j in segs]
    tri = causal.astype(F32).astype(BF16)
    acum = [_cumsum_rows(tri, -jnp.exp(hp_ref[0:1, :]) * x) for x in dt_all]
    acum_t = [_rows_as_lanes(x, n) for x in acum]
    dt_p = [_lane_spread(x, sel_p3_ref) for x in dt_all]
    ac_p = [_lane_spread(x, sel_p3_ref) for x in acum]
    ac_n = [_lane_spread(x, sel_n3_ref) for x in acum]
    xs = [x[:, :SSD_INNER] for x in xbc]
    xdt_all = _each(lambda x, d: x * d, xs, dt_p)
    x_out_all = _each(lambda x, a: x * jnp.exp(a[n - 1:n, :] - a), xdt_all, ac_p)
    skip_all = [d_ref[...] * x for x in xs]
    e_n = _each(jnp.exp, ac_n)
    pairs = [(j, g) for j in segs for g in range(SSD_GROUPS)]
    bm = [pad(xbc[j][:, SSD_INNER + g * SSD_N:SSD_INNER + (g + 1) * SSD_N]) for j, g in pairs]
    cm = [xbc[j][:, SSD_INNER + SSD_BC + g * SSD_N:SSD_INNER + SSD_BC + (g + 1) * SSD_N]
          for j, g in pairs]
    cb = _each(_mm_nt, cm, bm)

    units = [(j, h) for j in segs for h in range(nh)]
    grp = [j * SSD_GROUPS + h // heads_per_group for j, h in units]
    pcols = [slice(h * SSD_P, (h + 1) * SSD_P) for _, h in units]
    ncols = [slice(h * SSD_N, (h + 1) * SSD_N) for _, h in units]
    ar = [acum_t[j][SM_SSD_DT + h:SM_SSD_DT + h + 1, 0:n] for j, h in units]
    dec = [_masked_decay(ac_p[j][:, cs][:, :n] - rr, causal)
           for (j, _), cs, rr in zip(units, pcols, ar)]
    y_att = [_mm(cb[g] * d, xdt_all[j][:, cs]) for g, d, (j, _), cs in zip(grp, dec, units, pcols)]
    c_in = [cm[g] * e_n[j][:, cs] for g, (j, _), cs in zip(grp, units, ncols)]
    upd = [_mm_tn(x_out_all[j][:, cs], bm[g]) for g, (j, _), cs in zip(grp, units, pcols)]
    e_last = [e_n[j][n - 1:n, cs] for (j, _), cs in zip(units, ncols)]
    skip = [skip_all[j][:, cs] for (j, _), cs in zip(units, pcols)]

    def emit(idx, s):
        y = [y_att[i] + _mm_nt(c_in[i], ss) + skip[i] for i, ss in zip(idx, s)]
        y = [yy * _silu(z_views[units[i][0]][:, units[i][1] * SSD_P:(units[i][1] + 1) * SSD_P])
             for i, yy in zip(idx, y)]
        for p in sorted({grp[i] for i in idx}):
            mine = [(i, yy) for i, yy in zip(idx, y) if grp[i] == p]
            ms = sum(jnp.sum(yy * yy, axis=-1, keepdims=True) for _, yy in mine) * (1.0 / group_w)
            scale = lax.rsqrt(ms + NORM_EPS)
            for i, yy in mine:
                j, h = units[i]
                cols = slice(h * SSD_P, (h + 1) * SSD_P)
                o_views[j][:, cols] = (yy * scale) * nrm_ref[:, cols]

    step = lambda idx, s: [e_last[i] * ss + upd[i] for i, ss in zip(idx, s)]
    everything = list(range(m * nh))
    if chained:
        starts, s = [], s_in
        for j in segs:
            starts += s
            s = step(list(range(j * nh, (j + 1) * nh)), s)
        emit(everything, starts)
        return s
    starts = [s_in[j][h] for j, h in units]
    emit(everything, starts)
    s_new = step(everything, starts)
    return [s_new[j * nh:(j + 1) * nh] for j in segs]


def _scan_kernel(core, n_act, conv_idx, n_param, n_heads, n, m, nb, aliased):
    has_conv = conv_idx is not None
    chained = nb is None

    def kern(*refs):
        it = iter(refs)
        acts = [next(it) for _ in range(n_act)]
        hist = next(it) if has_conv else None
        s0 = next(it)
        if aliased:
            next(it)
        params = [next(it) for _ in range(n_param)]
        o_ref, s_ref = next(it), next(it)
        xp_ref = next(it) if has_conv else None

        if chained:
            @pl.when(pl.program_id(1) == 0)
            def _():
                s_ref[...] = s0[...]
                if has_conv:
                    xp_ref[0:SUBLANES, :] = hist[...]

            rows = lambda ref: [ref.at[pl.ds(j * n, n)] for j in range(m)]
            conv = None
            if has_conv:
                y = _conv_rows(xp_ref, acts[conv_idx], params[0], m * n)
                xp_ref[0:SUBLANES, :] = xp_ref[m * n:m * n + SUBLANES, :]
                conv = [y[j * n:(j + 1) * n] for j in range(m)]
            s_out = core([rows(a) for a in acts], conv, [s_ref[h] for h in range(n_heads)], params,
                         rows(o_ref), chained=True)
            for h in range(n_heads):
                s_ref[h] = s_out[h]
            return

        def step(i, carry):
            pick = lambda ref: [ref.at[i * m + j] for j in range(m)]
            conv = None
            if has_conv:
                conv = []
                for j, (hv, xv) in enumerate(zip(pick(hist), pick(acts[conv_idx]))):
                    xp = xp_ref.at[j]
                    xp[0:SUBLANES, :] = hv[...]
                    conv.append(_conv_rows(xp, xv, params[0], n))
            s_in = [[sv[h] for h in range(n_heads)] for sv in pick(s0)]
            s_out = core([pick(a) for a in acts], conv, s_in, params, pick(o_ref), chained=False)
            for sv, so in zip(pick(s_ref), s_out):
                for h in range(n_heads):
                    sv[h] = so[h]
            return carry

        lax.fori_loop(0, nb // m, step, 0)

    return kern


def _scan_call(name, core, acts, conv_idx, hist, s0, params, out_w, n, n_valid, layer, s_acc):
    bsz, t, _ = acts[0].shape
    nc = t // n
    state_shape = s0.shape[-3:]
    aliased = s_acc is not None
    if nc > 1:
        m = PROMPT_CHUNKS_PER_STEP[name] if nc % PROMPT_CHUNKS_PER_STEP[name] == 0 else 1
        nb, lead, grid = None, None, (bsz, nc // m)
    else:
        nb = DECODE_SEQS_PER_STEP if bsz % DECODE_SEQS_PER_STEP == 0 else 1
        m = DECODE_SEQS_TOGETHER if nb % DECODE_SEQS_TOGETHER == 0 else 1
        lead, grid = nb, (bsz // nb, 1)
    rows = n * m if nb is None else n

    act_spec = lambda w: pl.BlockSpec((lead, rows, w), lambda b, c: (b, c, 0))
    seq_spec = lambda shape: pl.BlockSpec((lead,) + tuple(shape), lambda b, c: (b,) + (0,) * len(shape))
    if layer is None:
        state_spec = seq_spec(state_shape)
    else:
        state_spec = pl.BlockSpec((None, lead) + tuple(state_shape),
                                  lambda b, c: (layer, b) + (0,) * len(state_shape))
    param_spec = lambda a: pl.BlockSpec(a.shape, lambda b, c: (0,) * a.ndim)

    has_conv = hist is not None
    operands = list(acts) + ([hist] if has_conv else []) + [s0] + ([s_acc] if aliased else [])
    in_specs = ([act_spec(a.shape[-1]) for a in acts]
                + ([seq_spec(hist.shape[1:])] if has_conv else []) + [state_spec]
                + ([pl.BlockSpec(memory_space=pl.ANY)] if aliased else []))
    alias = {len(operands) - 1: 1} if aliased else {}
    operands += list(params)
    in_specs += [param_spec(p) for p in params]
    conv_w = acts[conv_idx].shape[-1] if has_conv else None
    xp_shape = (SUBLANES + rows, conv_w) if nb is None else (m, SUBLANES + rows, conv_w)
    s_out_shape = s_acc.shape if aliased else s0.shape
    kern = _scan_kernel(functools.partial(core, n=n, n_valid=n_valid), len(acts), conv_idx,
                        len(params), state_shape[0], n, m, nb, aliased)
    return pl.pallas_call(
        kern,
        grid=grid,
        in_specs=in_specs,
        out_specs=[act_spec(out_w), state_spec],
        out_shape=[jax.ShapeDtypeStruct((bsz, t, out_w), F32), jax.ShapeDtypeStruct(s_out_shape, F32)],
        scratch_shapes=[pltpu.VMEM(xp_shape, F32)] if has_conv else [],
        input_output_aliases=alias,
        compiler_params=_cparams(2),
        name=f"{name}_{n}",
    )(*operands)


def _lane_row(pairs):
    row = jnp.zeros((SMALL_W,), F32)
    for off, vec in pairs:
        row = row.at[off:off + vec.shape[0]].set(vec.astype(F32))
    return row[None, :]


def _spread_matrix(blocks):
    total = sum(w for _, w in blocks)
    sel = np.zeros((SMALL_W, total), np.float32)
    col = 0
    for src, w in blocks:
        sel[src, col:col + w] = 1.0
        col += w
    return jnp.asarray(np.tile(sel, (3, 1)), BF16)


_W_IN_SIZES = (("gdn_qkv", GDN_CONV), ("gdn_z", GDN_V), ("gdn_a", GDN_HEADS), ("gdn_b", GDN_HEADS),
               ("gla_qk", 2 * GLA_QK), ("gla_v", GLA_V), ("gla_g", GLA_V), ("gla_r", GLA_RANK),
               ("ssd_z", SSD_INNER), ("ssd_xbc", SSD_CONV), ("ssd_dt", SSD_HEADS), ("gates", N_GATES))
W_IN_SRC = {}
for _name, _width in _W_IN_SIZES:
    W_IN_SRC[_name] = sum(w for _, w in _W_IN_SIZES[:len(W_IN_SRC)])
D_IN = sum(w for _, w in _W_IN_SIZES)
assert [W_IN_SRC[k] % SMALL_W for k in ("gdn_a", "gdn_b", "gla_r", "ssd_dt")] == [
    SM_GDN_A, SM_GDN_B, SM_GLA_R, SM_SSD_DT]
REGROUP_ROWS = 256


def _regroup_kernel(x_ref, o_ref):
    col = 0
    for name, width in PROJ_GROUPS:
        if name == "small":
            lane = _iota2(REGROUP_ROWS, SMALL_W, 1)
            tile = jnp.zeros((REGROUP_ROWS, SMALL_W), F32)
            for piece, w in (("gdn_a", 2 * GDN_HEADS), ("gla_r", GLA_RANK), ("ssd_dt", SSD_HEADS)):
                src = W_IN_SRC[piece]
                base = src - src % SMALL_W
                inside = (lane >= src - base) & (lane < src - base + w)
                tile = jnp.where(inside, x_ref[:, base:base + SMALL_W], tile)
            o_ref[:, col:col + width] = tile.astype(BF16)
        else:
            src = W_IN_SRC[name]
            o_ref[:, col:col + width] = x_ref[:, src:src + width].astype(BF16)
        col += width
    o_ref[:, D_PROJ:GATE_COL] = jnp.zeros((REGROUP_ROWS, GATE_COL - D_PROJ), BF16)
    src = W_IN_SRC["gates"]
    o_ref[:, GATE_COL:] = x_ref[:, src:src + N_GATES].astype(BF16)


def _regroup_w_in(w_all):
    depth = w_all.shape[0]
    out_w = GATE_COL + N_GATES
    return pl.pallas_call(
        _regroup_kernel,
        grid=(depth, D_MODEL // REGROUP_ROWS),
        in_specs=[pl.BlockSpec((None, REGROUP_ROWS, D_IN), lambda l, i: (l, i, 0))],
        out_specs=pl.BlockSpec((None, REGROUP_ROWS, out_w), lambda l, i: (l, i, 0)),
        out_shape=jax.ShapeDtypeStruct((depth, D_MODEL, out_w), BF16),
        compiler_params=_cparams(2),
        name="regroup_w_in",
    )(w_all)


def _prep_matrices(params):
    mats = dict(w_in=_regroup_w_in(params["w_in"].astype(F32)))
    for name in ("ffn1_w_gate", "ffn1_w_up", "ffn1_w_down", "ffn2_w_gate", "ffn2_w_up", "ffn2_w_down",
                 "w_br_gdn", "w_br_gla", "w_br_ssd", "w_out"):
        mats[name] = params[name].astype(BF16)
    return mats


def _prep_layer(p, mats):
    wup = jnp.zeros((SMALL_W, GLA_QK), F32).at[SM_GLA_R:SM_GLA_R + GLA_RANK].set(p["gla_gate_up"])
    row = lambda v: v.astype(F32)[None, :]
    pad_rows = lambda rws: jnp.concatenate(rws + [jnp.zeros((SUBLANES - len(rws), SMALL_W), F32)], 0)
    ffn = lambda i: (row(p[f"ffn{i}_norm_pre"]), mats[f"ffn{i}_w_gate"], mats[f"ffn{i}_w_up"],
                     mats[f"ffn{i}_w_down"], row(p[f"ffn{i}_norm_post"]))
    return dict(
        ffn1=ffn(1), ffn2=ffn(2),
        mix_norm_pre=row(p["mix_norm_pre"]), mix_norm_post=row(p["mix_norm_post"]), w_in=mats["w_in"],
        b_merge=p["b_merge"].astype(F32),
        gdn_cw=p["gdn_conv_w"].T.astype(F32),
        gdn_hp=pad_rows([_lane_row([(SM_GDN_A, p["gdn_A_log"])]),
                         _lane_row([(SM_GDN_A, p["gdn_dt_bias"])])]),
        gdn_norm=row(p["gdn_norm"]),
        gdn_sel=_spread_matrix([(SM_GDN_A + h, GDN_DV) for h in range(GDN_HEADS)]
                               + [(SM_GDN_B + h, GDN_DV) for h in range(GDN_HEADS)]),
        gla_wup=wup.astype(BF16), gla_gb=row(p["gla_gate_bias"]), gla_norm=row(p["gla_norm"]),
        ssd_cw=p["ssd_conv_w"].T.astype(F32), ssd_cb=row(p["ssd_conv_b"]),
        ssd_hp=pad_rows([_lane_row([(SM_SSD_DT, p["ssd_A_log"])]),
                         _lane_row([(SM_SSD_DT, p["ssd_dt_bias"])])]),
        ssd_norm=row(p["ssd_norm"]),
        ssd_sel_p=_spread_matrix([(SM_SSD_DT + h, SSD_P) for h in range(SSD_HEADS)]),
        ssd_sel_n=_spread_matrix([(SM_SSD_DT + h, SSD_N) for h in range(SSD_HEADS)]),
        ssd_d=row(jnp.repeat(p["ssd_D"], SSD_P)),
        w_br_gdn=mats["w_br_gdn"], w_br_gla=mats["w_br_gla"], w_br_ssd=mats["w_br_ssd"],
        w_out=mats["w_out"])


def _layer(x, state, w, wl, bsz, t, n, n_valid, layer=None, s_acc=(None, None, None)):
    gdn_hist, s_gdn, s_gla, ssd_hist, s_ssd = state
    x = _ffn(x, *w["ffn1"], wl)
    proj = dict(zip((name for name, _ in PROJ_GROUPS), _inproj(x, w["mix_norm_pre"], w["w_in"], wl)))
    seq = lambda a: a.reshape(bsz, t, a.shape[-1])
    sm = seq(proj["small"])
    t_valid = t - (n - n_valid)
    tails = [seq(proj[g])[:, t_valid - (CONV_WIDTH - 1):t_valid] for g in CONV_GROUPS]
    o_gdn, s_gdn_new = _scan_call(
        "gdn", _gdn_core, [seq(proj["gdn_qkv"]), seq(proj["gdn_z"]), sm], 0, gdn_hist, s_gdn,
        [w["gdn_cw"], w["gdn_hp"], w["gdn_norm"], w["gdn_sel"]], GDN_V, n, n_valid, layer, s_acc[0])
    o_gla, s_gla_new = _scan_call(
        "gla", _gla_core, [seq(proj["gla_qk"]), seq(proj["gla_v"]), seq(proj["gla_g"]), sm], None,
        None, s_gla, [w["gla_wup"], w["gla_gb"], w["gla_norm"]], GLA_V, n, n_valid, layer, s_acc[1])
    o_ssd, s_ssd_new = _scan_call(
        "ssd", _ssd_core, [seq(proj["ssd_z"]), seq(proj["ssd_xbc"]), sm], 1, ssd_hist, s_ssd,
        [w["ssd_cw"], w["ssd_cb"], w["ssd_hp"], w["ssd_norm"], w["ssd_sel_p"], w["ssd_sel_n"],
         w["ssd_d"]], SSD_INNER, n, n_valid, layer, s_acc[2])
    flat = lambda a: a.reshape(bsz * t, a.shape[-1])
    x = _merge(x, flat(o_gdn), flat(o_gla), flat(o_ssd), w["mix_norm_pre"], w["w_in"], w["b_merge"],
               w["w_br_gdn"], w["w_br_gla"], w["w_br_ssd"], w["w_out"], w["mix_norm_post"], wl)
    x = _ffn(x, *w["ffn2"], wl)
    return x, (tails[0], s_gdn_new, s_gla_new, tails[1], s_ssd_new)


def kernel(x_prompt, x_sample, state_gdn_conv, state_gdn, state_gla, state_ssd_conv, state_ssd,
           ffn1_norm_pre, ffn1_norm_post, ffn1_w_gate, ffn1_w_up, ffn1_w_down,
           mix_norm_pre, mix_norm_post, w_in, b_merge,
           gdn_conv_w, gdn_A_log, gdn_dt_bias, gdn_norm,
           gla_gate_up, gla_gate_bias, gla_norm,
           ssd_conv_w, ssd_conv_b, ssd_A_log, ssd_dt_bias, ssd_D, ssd_norm,
           w_br_gdn, w_br_gla, w_br_ssd, w_out,
           ffn2_norm_pre, ffn2_norm_post, ffn2_w_gate, ffn2_w_up, ffn2_w_down):
    params = dict(
        ffn1_norm_pre=ffn1_norm_pre, ffn1_norm_post=ffn1_norm_post, ffn1_w_gate=ffn1_w_gate,
        ffn1_w_up=ffn1_w_up, ffn1_w_down=ffn1_w_down,
        mix_norm_pre=mix_norm_pre, mix_norm_post=mix_norm_post, w_in=w_in, b_merge=b_merge,
        gdn_conv_w=gdn_conv_w, gdn_A_log=gdn_A_log, gdn_dt_bias=gdn_dt_bias, gdn_norm=gdn_norm,
        gla_gate_up=gla_gate_up, gla_gate_bias=gla_gate_bias, gla_norm=gla_norm,
        ssd_conv_w=ssd_conv_w, ssd_conv_b=ssd_conv_b, ssd_A_log=ssd_A_log, ssd_dt_bias=ssd_dt_bias,
        ssd_D=ssd_D, ssd_norm=ssd_norm,
        w_br_gdn=w_br_gdn, w_br_gla=w_br_gla, w_br_ssd=w_br_ssd, w_out=w_out,
        ffn2_norm_pre=ffn2_norm_pre, ffn2_norm_post=ffn2_norm_post, ffn2_w_gate=ffn2_w_gate,
        ffn2_w_up=ffn2_w_up, ffn2_w_down=ffn2_w_down)
    bp, tp, _ = x_prompt.shape
    bs, ts, _ = x_sample.shape
    ts_pad = SUBLANES
    n_p = min(CHUNK, tp)
    depth = state_gdn.shape[0]

    prompt_init = (jnp.zeros((bp, SUBLANES, GDN_CONV), F32),
                   jnp.zeros((bp, GDN_HEADS, GDN_DK, GDN_DV), F32),
                   jnp.zeros((bp, GLA_HEADS, GLA_DK, GLA_DV), F32),
                   jnp.zeros((bp, SUBLANES, SSD_CONV), F32),
                   jnp.zeros((bp, SSD_HEADS, SSD_P, SSD_N), F32))
    hist_rows = lambda s: jnp.pad(s.astype(F32), ((0, 0), (SUBLANES - (CONV_WIDTH - 1), 0), (0, 0)))
    sample_states = tuple(s.astype(F32) for s in (state_gdn, state_gla, state_ssd))
    s_acc = tuple(jnp.zeros(s.shape, F32) for s in sample_states)

    y_p = x_prompt.reshape(bp * tp, D_MODEL)
    y_s = jnp.pad(x_sample, ((0, 0), (0, ts_pad - ts), (0, 0))).reshape(bs * ts_pad, D_MODEL)
    new_p, s_conv = [], []
    mats = _prep_matrices(params)
    small = {name: arr for name, arr in params.items() if name not in mats and name != "w_in"}
    for l in range(depth):
        w = _prep_layer({name: arr[l] for name, arr in small.items()}, mats)
        y_p, st_p = _layer(y_p, prompt_init, w, l, bp, tp, n_p, n_p)
        sample_state = (hist_rows(state_gdn_conv[l]), sample_states[0], sample_states[1],
                        hist_rows(state_ssd_conv[l]), sample_states[2])
        y_s, st_s = _layer(y_s, sample_state, w, l, bs, ts_pad, ts_pad, ts, layer=l, s_acc=s_acc)
        new_p.append(st_p)
        s_conv.append((st_s[0], st_s[3]))
        s_acc = (st_s[1], st_s[2], st_s[4])
    p_out = tuple(jnp.stack(a) for a in zip(*new_p))
    s_gdn_conv, s_ssd_conv = (jnp.stack(a) for a in zip(*s_conv))
    y_prompt = y_p.reshape(bp, tp, D_MODEL)
    y_sample = y_s.reshape(bs, ts_pad, D_MODEL)[:, :ts]
    return (y_prompt, y_sample) + p_out + (s_gdn_conv, s_acc[0], s_acc[1], s_ssd_conv, s_acc[2])
```
